```python
import jax, jax.numpy as jnp
from jax import lax
import numpy as np

D_MODEL = 2048
BATCH = 1
SEQ = 16384
DEPTH = 1

GRID_W = 64
NA_HEAD_DIM = 64
NA_WIDTH = D_MODEL // 2
NA_HEADS = NA_WIDTH // NA_HEAD_DIM
NA_WIN_ROWS = 8
NA_WIN_COLS = 16
HG_KEY_DIM = 128
HG_VAL_DIM = 128
HG_WIDTH = D_MODEL // 2
HG_HEADS = HG_WIDTH // HG_VAL_DIM
HG_KEY_WIDTH = HG_HEADS * HG_KEY_DIM
HG_CHUNK = 64
N_BRANCHES = 2
NORM_EPS = 1e-6
IN_COLS = 4 * NA_WIDTH + 3 * HG_KEY_WIDTH + 2 * HG_WIDTH + N_BRANCHES * D_MODEL

kernel_name = "hybrid_natten_hgrn2_gated_merge_encoder"


def _rmsnorm(x, w):
    xf = x.astype(jnp.float32)
    y = xf * lax.rsqrt(jnp.mean(xf * xf, axis=-1, keepdims=True) + NORM_EPS)
    return (y * w.astype(jnp.float32)).astype(x.dtype)


def _neighbourhood_attention(q, k, v, rpb):
    B, T, H, Dh = q.shape
    rows = T // GRID_W
    wr = min(NA_WIN_ROWS, rows)
    qg = q.reshape(B, rows, GRID_W, H, Dh)
    kg = k.reshape(B, rows, GRID_W, H, Dh)
    vg = v.reshape(B, rows, GRID_W, H, Dh)
    qc = jnp.arange(GRID_W)
    kc = jnp.arange(GRID_W)
    cs = jnp.clip(qc - NA_WIN_COLS // 2, 0, GRID_W - NA_WIN_COLS)
    col_mask = (kc[None, :] >= cs[:, None]) & (kc[None, :] < cs[:, None] + NA_WIN_COLS)
    col_idx = jnp.clip(kc[None, :] - qc[:, None] + NA_WIN_COLS - 1, 0, 2 * NA_WIN_COLS - 2)
    scale = Dh ** -0.5

    def row_block(r):
        rs = jnp.clip(r - wr // 2, 0, rows - wr)
        kb = lax.dynamic_slice_in_dim(kg, rs, wr, axis=1)
        vb = lax.dynamic_slice_in_dim(vg, rs, wr, axis=1)
        qr = lax.dynamic_index_in_dim(qg, r, axis=1, keepdims=False)
        s = jnp.einsum('bqhd,bjkhd->bhqjk', qr, kb).astype(jnp.float32) * scale
        row_idx = rs + jnp.arange(wr) - r + NA_WIN_ROWS - 1
        bias = rpb[:, row_idx][:, :, col_idx]
        s = s + jnp.transpose(bias, (0, 2, 1, 3)).astype(jnp.float32)[None]
        s = jnp.where(col_mask[None, None, :, None, :], s, -jnp.inf)
        p = jax.nn.softmax(s.reshape(B, H, GRID_W, wr * GRID_W), axis=-1).astype(v.dtype)
        return jnp.einsum('bhqn,bnhd->bqhd', p, vb.reshape(B, wr * GRID_W, H, Dh))

    o = lax.map(row_block, jnp.arange(rows))
    return jnp.transpose(o, (1, 0, 2, 3, 4)).reshape(B, T, H, Dh)


def _gated_linear_scan(q, k, v, log_f):
    B, T, H, Dk = q.shape
    Dv = v.shape[-1]
    C = HG_CHUNK
    N = T // C

    def to_chunks(a):
        return a.reshape(B, N, C, H, a.shape[-1]).transpose(1, 0, 3, 2, 4)

    causal = jnp.tril(jnp.ones((C, C), dtype=bool))

    def step(S, inp):
        qb, kb, vb, gb = inp
        A = jnp.cumsum(gb, axis=2)
        diff = A[:, :, :, None, :] - A[:, :, None, :, :]
        decay = jnp.exp(jnp.where(causal[:, :, None], diff, -jnp.inf))
        scores = jnp.sum(qb[:, :, :, None, :] * kb[:, :, None, :, :] * decay, axis=-1)
        o = jnp.einsum('bhts,bhsv->bhtv', scores, vb) + \
            jnp.einsum('bhtd,bhdv->bhtv', qb * jnp.exp(A), S)
        A_last = A[:, :, -1, :]
        S = jnp.exp(A_last)[..., None] * S + \
            jnp.einsum('bhsd,bhsv->bhdv', kb * jnp.exp(A_last[:, :, None, :] - A), vb)
        return S, o

    S0 = jnp.zeros((B, H, Dk, Dv), jnp.float32)
    _, o = lax.scan(step, S0, (to_chunks(q), to_chunks(k), to_chunks(v), to_chunks(log_f)))
    return o.transpose(1, 0, 3, 2, 4).reshape(B, T, H, Dv)


def _hgrn2_bidirectional(q, f_fwd, f_bwd, i, lb):
    q = q.astype(jnp.float32)
    i = i.astype(jnp.float32)

    def gates(fl, lbd):
        fl = fl.astype(jnp.float32)
        f = lbd + (1.0 - lbd) * jax.nn.sigmoid(fl)
        return jnp.log(f), (1.0 - lbd) * jax.nn.sigmoid(-fl)

    lf_f, k_f = gates(f_fwd, lb[0])
    lf_b, k_b = gates(f_bwd, lb[1])
    o_fwd = _gated_linear_scan(q, k_f, i, lf_f)
    flip = lambda a: jnp.flip(a, axis=1)
    o_bwd = flip(_gated_linear_scan(flip(q), flip(k_b), flip(i), flip(lf_b)))
    return o_fwd + o_bwd


def setup_inputs(seed: int = 0) -> dict:
    key = jax.random.key(seed)
    ks = jax.random.split(key, 11)
    f32 = jnp.float32
    nrm = lambda k, s: jax.random.normal(k, s, f32)
    x = nrm(ks[0], (BATCH, SEQ, D_MODEL))
    norm_pre_w = 1.0 + 0.02 * nrm(ks[1], (DEPTH, D_MODEL))
    w_in = nrm(ks[2], (DEPTH, D_MODEL, IN_COLS)) * D_MODEL ** -0.5
    b_gate = 0.02 * nrm(ks[3], (DEPTH, N_BRANCHES * D_MODEL))
    na_rel_bias = 0.1 * nrm(ks[4], (DEPTH, NA_HEADS, 2 * NA_WIN_ROWS - 1, 2 * NA_WIN_COLS - 1))
    hg_lb_logits = 0.5 * nrm(ks[5], (DEPTH + 1, 2, HG_KEY_WIDTH))
    hg_norm_w = 1.0 + 0.02 * nrm(ks[6], (DEPTH, HG_WIDTH))
    w_branch = nrm(ks[7], (DEPTH, NA_WIDTH + HG_WIDTH, D_MODEL)) * NA_WIDTH ** -0.5
    w_out = nrm(ks[8], (DEPTH, D_MODEL, D_MODEL)) * D_MODEL ** -0.5
    norm_post_w = 1.0 + 0.02 * nrm(ks[9], (DEPTH, D_MODEL))
    return {"x": x, "norm_pre_w": norm_pre_w, "w_in": w_in, "b_gate": b_gate,
            "na_rel_bias": na_rel_bias, "hg_lb_logits": hg_lb_logits, "hg_norm_w": hg_norm_w,
            "w_branch": w_branch, "w_out": w_out, "norm_post_w": norm_post_w}


def reference(x, norm_pre_w, w_in, b_gate, na_rel_bias, hg_lb_logits, hg_norm_w, w_branch, w_out, norm_post_w):
    B, T, _ = x.shape
    lb_all = jnp.cumsum(jax.nn.softmax(hg_lb_logits.astype(jnp.float32), axis=0), axis=0)
    splits = [int(s) for s in np.cumsum([NA_WIDTH] * 4 + [HG_KEY_WIDTH] * 3 + [HG_WIDTH] * 2)]
    for l in range(DEPTH):
        xn = _rmsnorm(x, norm_pre_w[l])
        h = xn @ w_in[l]
        na_q, na_k, na_v, na_z, hg_q, hg_ff, hg_fb, hg_i, hg_z, gate_pre = jnp.split(h, splits, axis=-1)

        ha = lambda a: a.reshape(B, T, NA_HEADS, NA_HEAD_DIM)
        o_a = _neighbourhood_attention(ha(na_q), ha(na_k), ha(na_v), na_rel_bias[l]).reshape(B, T, NA_WIDTH)
        o_a = o_a * jax.nn.silu(na_z)

        hk = lambda a: a.reshape(B, T, HG_HEADS, HG_KEY_DIM)
        lb = lb_all[l].reshape(2, HG_HEADS, HG_KEY_DIM)
        o_b = _hgrn2_bidirectional(hk(hg_q), hk(hg_ff), hk(hg_fb), hg_i.reshape(B, T, HG_HEADS, HG_VAL_DIM), lb)
        o_b = _rmsnorm(o_b, hg_norm_w[l].reshape(HG_HEADS, HG_VAL_DIM)).reshape(B, T, HG_WIDTH).astype(x.dtype)
        o_b = o_b * jax.nn.silu(hg_z)

        g = jax.nn.sigmoid(gate_pre + b_gate[l])
        g_a, g_b = jnp.split(g, N_BRANCHES, axis=-1)
        y = g_a * (o_a @ w_branch[l, :NA_WIDTH]) + g_b * (o_b @ w_branch[l, NA_WIDTH:])
        x = x + _rmsnorm(y @ w_out[l], norm_post_w[l])
    return x
```

```python
import functools

import jax
import jax.numpy as jnp
from jax import lax
from jax.experimental import pallas as pl
from jax.experimental.pallas import tpu as pltpu

D_MODEL = 2048
GRID_W = 64
NA_HEAD_DIM = 64
NA_WIDTH = 1024
NA_HEADS = 16
NA_WIN_ROWS = 8
NA_WIN_COLS = 16
HG_DIM = 128
HG_HEADS = 8
HG_WIDTH = 1024
NORM_EPS = 1e-6
QKV_COLS = 3 * NA_WIDTH
REST_COLS = NA_WIDTH + 5 * HG_WIDTH + 2 * D_MODEL
MASK_VALUE = -1e30

VMEM_LIMIT_BYTES = 56 * 1024 * 1024


def _params(*sem):
    return pltpu.CompilerParams(dimension_semantics=sem, vmem_limit_bytes=VMEM_LIMIT_BYTES)


def _rmsnorm_kernel(x_ref, w_ref, o_ref):
    x = x_ref[...]
    ms = jnp.mean(x * x, axis=-1, keepdims=True)
    o_ref[...] = (x * lax.rsqrt(ms + NORM_EPS) * w_ref[...]).astype(o_ref.dtype)


def _rmsnorm(x, w, tm=512):
    t, d = x.shape
    return pl.pallas_call(
        _rmsnorm_kernel,
        grid=(t // tm,),
        in_specs=[pl.BlockSpec((tm, d), lambda i: (i, 0)),
                  pl.BlockSpec((1, d), lambda i: (0, 0))],
        out_specs=pl.BlockSpec((tm, d), lambda i: (i, 0)),
        out_shape=jax.ShapeDtypeStruct((t, d), jnp.bfloat16),
        compiler_params=_params("parallel"),
        name="rmsnorm_pre",
    )(x, w.reshape(1, d))


def _matmul_kernel(a_ref, b_ref, o_ref):
    o_ref[...] = jnp.dot(a_ref[...], b_ref[...],
                         preferred_element_type=jnp.float32).astype(o_ref.dtype)


def _matmul(a, b, out_dtype, name, tm=1024, tn=1024):
    m, k = a.shape
    _, n = b.shape
    return pl.pallas_call(
        _matmul_kernel,
        grid=(m // tm, n // tn),
        in_specs=[pl.BlockSpec((tm, k), lambda i, j: (i, 0)),
                  pl.BlockSpec((k, tn), lambda i, j: (0, j))],
        out_specs=pl.BlockSpec((tm, tn), lambda i, j: (i, j)),
        out_shape=jax.ShapeDtypeStruct((m, n), out_dtype),
        compiler_params=_params("parallel", "arbitrary"),
        name=name,
    )(a, b)


def _na_bias_table(rpb):
    qc = jnp.arange(GRID_W)
    kc = jnp.arange(GRID_W)
    cs = jnp.clip(qc - NA_WIN_COLS // 2, 0, GRID_W - NA_WIN_COLS)
    col_mask = (kc[None, :] >= cs[:, None]) & (kc[None, :] < cs[:, None] + NA_WIN_COLS)
    col_idx = jnp.clip(kc[None, :] - qc[:, None] + NA_WIN_COLS - 1, 0, 2 * NA_WIN_COLS - 2)
    var = jnp.arange(NA_WIN_ROWS)
    j = jnp.arange(NA_WIN_ROWS)
    row_idx = j[None, :] - var[:, None] + NA_WIN_ROWS - 1
    b = rpb[:, row_idx][:, :, :, col_idx]
    b = jnp.where(col_mask[None, None, None], b, MASK_VALUE)
    b = jnp.transpose(b, (1, 0, 3, 2, 4))
    return b.reshape(NA_WIN_ROWS, NA_HEADS, GRID_W, NA_WIN_ROWS * GRID_W).astype(jnp.float32)


def _na_kernel(q_ref, k_ref, v_ref, z_ref, bias_ref, o_ref):
    scale = NA_HEAD_DIM ** -0.5
    outs = []
    for h in range(NA_HEADS):
        sl = slice(h * NA_HEAD_DIM, (h + 1) * NA_HEAD_DIM)
        qh = q_ref[:, sl] * scale
        s = lax.dot_general(qh, k_ref[:, sl], (((1,), (1,)), ((), ())),
                            preferred_element_type=jnp.float32)
        s = s + bias_ref[0, h]
        m = jnp.max(s, axis=-1, keepdims=True)
        p = jnp.exp(s - m)
        l = jnp.sum(p, axis=-1, keepdims=True)
        o = jnp.dot(p.astype(jnp.bfloat16), v_ref[:, sl], preferred_element_type=jnp.float32)
        outs.append(o / l)
    o_all = jnp.concatenate(outs, axis=-1)
    z = z_ref[...]
    o_ref[...] = (o_all * (z * jax.nn.sigmoid(z))).astype(o_ref.dtype)


def _neighbourhood_attention(qkv, rest, bias_tab):
    t = qkv.shape[0]
    rows = t // GRID_W
    band = NA_WIN_ROWS * GRID_W

    def row_start(r):
        return jnp.clip(r - NA_WIN_ROWS // 2, 0, rows - NA_WIN_ROWS)

    return pl.pallas_call(
        _na_kernel,
        grid=(rows,),
        in_specs=[
            pl.BlockSpec((GRID_W, NA_WIDTH), lambda r: (r, 0)),
            pl.BlockSpec((pl.Element(band), pl.Element(NA_WIDTH)),
                         lambda r: (row_start(r) * GRID_W, NA_WIDTH)),
            pl.BlockSpec((pl.Element(band), pl.Element(NA_WIDTH)),
                         lambda r: (row_start(r) * GRID_W, 2 * NA_WIDTH)),
            pl.BlockSpec((GRID_W, NA_WIDTH), lambda r: (r, 0)),
            pl.BlockSpec((1, NA_HEADS, GRID_W, band), lambda r: (r - row_start(r), 0, 0, 0)),
        ],
        out_specs=pl.BlockSpec((GRID_W, NA_WIDTH), lambda r: (r, 0)),
        out_shape=jax.ShapeDtypeStruct((t, NA_WIDTH), jnp.bfloat16),
        compiler_params=_params("arbitrary"),
        name="neighbourhood_attention",
    )(qkv, qkv, qkv, rest, bias_tab)


HG_CHUNK = 64
HG_BLOCK = 512


def _hgrn_chunk(q, fl, v, lb, st, reverse):
    c = q.shape[0]
    f = lb + (1.0 - lb) * jax.nn.sigmoid(fl)
    g = jnp.log(f)
    k = (1.0 - lb) * jax.nn.sigmoid(-fl)

    row = lax.broadcasted_iota(jnp.int32, (c, HG_DIM), 0)
    pos = (c - 1 - row) if reverse else row

    def from_prev(x, m):
        return pltpu.roll(x, (c - m) if reverse else m, 0)

    def from_next(x, m):
        return pltpu.roll(x, m if reverse else (c - m), 0)

    a = g
    step = 1
    while step < c:
        a = a + jnp.where(pos >= step, from_prev(a, step), 0.0)
        step *= 2

    ti = lax.broadcasted_iota(jnp.int32, (c, c), 0)
    si = lax.broadcasted_iota(jnp.int32, (c, c), 1)
    if reverse:
        ti, si = c - 1 - ti, c - 1 - si
    xor = ti ^ si
    later = ti > si

    def qk(qm, km):
        return lax.dot_general(qm.astype(jnp.bfloat16), km.astype(jnp.bfloat16),
                               (((1,), (1,)), ((), ())), preferred_element_type=jnp.float32)

    scores = jnp.where(ti == si, qk(q, k), 0.0)
    u = a
    m = 1
    while m < c:
        lower = (pos & (2 * m - 1)) < m
        r = jnp.where(lower, u, from_prev(u, m))
        u = jnp.where(lower, from_next(u, m), u)
        e = jnp.exp(-jnp.abs(a - r))
        scores = scores + jnp.where(later & (xor >= m) & (xor < 2 * m), qk(q * e, k * e), 0.0)
        m *= 2
    q_in = (q * jnp.exp(a)).astype(jnp.bfloat16)
    k_out = (k * jnp.exp(u - a)).astype(jnp.bfloat16)
    vb = v.astype(jnp.bfloat16)
    o = jnp.dot(scores.astype(jnp.bfloat16), vb, preferred_element_type=jnp.float32)
    o = o + lax.dot_general(q_in, st.astype(jnp.bfloat16), (((1,), (1,)), ((), ())),
                            preferred_element_type=jnp.float32)
    st_new = st * jnp.exp(u[0:1, :]) + jnp.dot(v.T.astype(jnp.bfloat16), k_out,
                                                preferred_element_type=jnp.float32)
    return o, st_new


def _hgrn_kernel(lbl_ref, qf_ref, ff_ref, vf_ref, qb_ref, fb_ref, vb_ref,
                 of_ref, ob_ref, st_ref, *, layer):
    @pl.when(pl.program_id(1) == 0)
    def _():
        st_ref[...] = jnp.zeros_like(st_ref)

    lg = lbl_ref[...]
    mx = jnp.max(lg, axis=0, keepdims=True)
    ex = jnp.exp(lg - mx)
    lb = jnp.sum(ex[: layer + 1], axis=0) / jnp.sum(ex, axis=0)
    lb_f, lb_b = lb[0:1, :], lb[1:2, :]

    n = HG_BLOCK // HG_CHUNK

    def body(ci, carry):
        sf, sb = carry
        rf = pl.multiple_of(ci * HG_CHUNK, HG_CHUNK)
        rb = pl.multiple_of((n - 1 - ci) * HG_CHUNK, HG_CHUNK)
        o_f, sf = _hgrn_chunk(qf_ref[pl.ds(rf, HG_CHUNK), :], ff_ref[pl.ds(rf, HG_CHUNK), :],
                              vf_ref[pl.ds(rf, HG_CHUNK), :], lb_f, sf, False)
        of_ref[pl.ds(rf, HG_CHUNK), :] = o_f
        o_b, sb = _hgrn_chunk(qb_ref[pl.ds(rb, HG_CHUNK), :], fb_ref[pl.ds(rb, HG_CHUNK), :],
                              vb_ref[pl.ds(rb, HG_CHUNK), :], lb_b, sb, True)
        ob_ref[pl.ds(rb, HG_CHUNK), :] = o_b
        return sf, sb

    sf, sb = lax.fori_loop(0, n, body, (st_ref[0], st_ref[1]))
    st_ref[0] = sf
    st_ref[1] = sb


def _hgrn2(rest, lb_logits, layer):
    t = rest.shape[0]
    nb = t // HG_BLOCK
    layers = lb_logits.shape[0]
    col = lambda seg: seg * HG_HEADS
    q_c, ff_c, fb_c, i_c = col(1), col(2), col(3), col(4)
    blk = (HG_BLOCK, HG_DIM)
    fwd = lambda c0: pl.BlockSpec(blk, lambda h, b: (b, c0 + h))
    bwd = lambda c0: pl.BlockSpec(blk, lambda h, b: (nb - 1 - b, c0 + h))
    return pl.pallas_call(
        functools.partial(_hgrn_kernel, layer=layer),
        grid=(HG_HEADS, nb),
        in_specs=[pl.BlockSpec((layers, 2, HG_DIM), lambda h, b: (0, 0, h)),
                  fwd(q_c), fwd(ff_c), fwd(i_c), bwd(q_c), bwd(fb_c), bwd(i_c)],
        out_specs=[pl.BlockSpec(blk, lambda h, b: (b, h)),
                   pl.BlockSpec(blk, lambda h, b: (nb - 1 - b, h))],
        out_shape=[jax.ShapeDtypeStruct((t, HG_WIDTH), jnp.float32)] * 2,
        scratch_shapes=[pltpu.VMEM((2, HG_DIM, HG_DIM), jnp.float32)],
        compiler_params=_params("parallel", "arbitrary"),
        name="hgrn2_bidirectional",
    )(lb_logits, rest, rest, rest, rest, rest, rest)


def _merge_kernel(x_ref, oa_ref, of_ref, ob_ref, zb_ref, ga_ref, gb_ref, bga_ref, bgb_ref,
                  hgw_ref, wa_ref, wb_ref, wo_ref, pw_ref, o_ref):
    zb = zb_ref[...]
    gate_b = zb * jax.nn.sigmoid(zb)
    osum = of_ref[...] + ob_ref[...]
    parts = []
    for h in range(HG_HEADS):
        sl = slice(h * HG_DIM, (h + 1) * HG_DIM)
        oh = osum[:, sl]
        ms = jnp.mean(oh * oh, axis=-1, keepdims=True)
        parts.append(oh * lax.rsqrt(ms + NORM_EPS) * hgw_ref[:, sl])
    o_b = (jnp.concatenate(parts, axis=-1) * gate_b).astype(jnp.bfloat16)

    pa = jnp.dot(oa_ref[...], wa_ref[...], preferred_element_type=jnp.float32)
    pb = jnp.dot(o_b, wb_ref[...], preferred_element_type=jnp.float32)
    y = (jax.nn.sigmoid(ga_ref[...] + bga_ref[...]) * pa
         + jax.nn.sigmoid(gb_ref[...] + bgb_ref[...]) * pb)
    u = jnp.dot(y.astype(jnp.bfloat16), wo_ref[...], preferred_element_type=jnp.float32)
    ms = jnp.mean(u * u, axis=-1, keepdims=True)
    o_ref[...] = x_ref[...] + u * lax.rsqrt(ms + NORM_EPS) * pw_ref[...]


def _merge(x, oa, o_f, o_b, rest, b_gate, hg_norm_w, w_a, w_b, w_o, post_w, tm=256):
    t, d = x.shape
    seg = lambda s: s
    tile = lambda w, c: pl.BlockSpec((tm, w), lambda i: (i, c))
    const = lambda shape: pl.BlockSpec(shape, lambda i: (0, 0), pipeline_mode=pl.Buffered(1))
    ga_col = (NA_WIDTH + 5 * HG_WIDTH) // D_MODEL
    return pl.pallas_call(
        _merge_kernel,
        grid=(t // tm,),
        in_specs=[tile(d, 0), tile(NA_WIDTH, 0), tile(HG_WIDTH, 0), tile(HG_WIDTH, 0),
                  tile(HG_WIDTH, seg(5)), tile(d, ga_col), tile(d, ga_col + 1),
                  const((1, d)), const((1, d)), const((1, HG_WIDTH)),
                  const((NA_WIDTH, d)), const((HG_WIDTH, d)), const((d, d)), const((1, d))],
        out_specs=tile(d, 0),
        out_shape=jax.ShapeDtypeStruct((t, d), jnp.float32),
        compiler_params=_params("parallel"),
        name="merge_out_proj",
    )(x, oa, o_f, o_b, rest, rest, rest, b_gate[:d].reshape(1, d), b_gate[d:].reshape(1, d),
      hg_norm_w.reshape(1, HG_WIDTH), w_a, w_b, w_o, post_w.reshape(1, d))


def kernel(x, norm_pre_w, w_in, b_gate, na_rel_bias, hg_lb_logits, hg_norm_w, w_branch, w_out, norm_post_w):
    b, t, d = x.shape
    depth = w_in.shape[0]
    bf16 = jnp.bfloat16
    outs = []
    for bi in range(b):
        xb = x[bi]
        for l in range(depth):
            w_in_l = w_in[l].astype(bf16)
            xn = _rmsnorm(xb, norm_pre_w[l])
            qkv = _matmul(xn, w_in_l[:, :QKV_COLS], bf16, "in_proj_qkv")
            rest = _matmul(xn, w_in_l[:, QKV_COLS:], jnp.float32, "in_proj_rest")
            oa = _neighbourhood_attention(qkv, rest, _na_bias_table(na_rel_bias[l]))
            o_f, o_b = _hgrn2(rest, hg_lb_logits.astype(jnp.float32), l)
            wb = w_branch[l].astype(bf16)
            xb = _merge(xb, oa, o_f, o_b, rest, b_gate[l], hg_norm_w[l],
                        wb[:NA_WIDTH], wb[NA_WIDTH:], w_out[l].astype(bf16), norm_post_w[l])
        outs.append(xb)
    return jnp.stack(outs, axis=0)
```

```python
import functools

import jax
import jax.numpy as jnp
import numpy as np
from jax import lax
from jax.experimental import pallas as pl
from jax.experimental.pallas import tpu as pltpu

D_MODEL = 2048
GRID_W = 64
NA_HEAD_DIM = 64
NA_WIDTH = 1024
NA_HEADS = 16
NA_WIN_ROWS = 8
NA_WIN_COLS = 16
HG_DIM = 128
HG_HEADS = 8
HG_WIDTH = 1024
NORM_EPS = 1e-6
QKV_COLS = 3 * NA_WIDTH
REST_COLS = NA_WIDTH + 5 * HG_WIDTH + 2 * D_MODEL
MASK_VALUE = -1e30

VMEM_LIMIT_BYTES = 56 * 1024 * 1024


def _params(*sem):
    return pltpu.CompilerParams(dimension_semantics=sem, vmem_limit_bytes=VMEM_LIMIT_BYTES)


def _rmsnorm_kernel(x_ref, w_ref, o_ref):
    x = x_ref[...]
    ms = jnp.mean(x * x, axis=-1, keepdims=True)
    o_ref[...] = (x * lax.rsqrt(ms + NORM_EPS) * w_ref[...]).astype(o_ref.dtype)


def _rmsnorm(x, w, tm=512):
    t, d = x.shape
    return pl.pallas_call(
        _rmsnorm_kernel,
        grid=(t // tm,),
        in_specs=[pl.BlockSpec((tm, d), lambda i: (i, 0)),
                  pl.BlockSpec((1, d), lambda i: (0, 0))],
        out_specs=pl.BlockSpec((tm, d), lambda i: (i, 0)),
        out_shape=jax.ShapeDtypeStruct((t, d), jnp.bfloat16),
        compiler_params=_params("parallel"),
        name="rmsnorm_pre",
    )(x, w.reshape(1, d))


def _matmul_kernel(a_ref, b_ref, o_ref):
    o_ref[...] = jnp.dot(a_ref[...], b_ref[...],
                         preferred_element_type=jnp.float32).astype(o_ref.dtype)


def _matmul(a, b, out_dtype, name, tm=1024, tn=1024):
    m, k = a.shape
    _, n = b.shape
    return pl.pallas_call(
        _matmul_kernel,
        grid=(m // tm, n // tn),
        in_specs=[pl.BlockSpec((tm, k), lambda i, j: (i, 0)),
                  pl.BlockSpec((k, tn), lambda i, j: (0, j))],
        out_specs=pl.BlockSpec((tm, tn), lambda i, j: (i, j)),
        out_shape=jax.ShapeDtypeStruct((m, n), out_dtype),
        compiler_params=_params("parallel", "arbitrary"),
        name=name,
    )(a, b)


def _na_bias_table(rpb):
    qc = np.arange(GRID_W)[:, None]
    kc = np.arange(GRID_W)[None, :]
    cs = np.clip(qc - NA_WIN_COLS // 2, 0, GRID_W - NA_WIN_COLS)
    col_mask = (kc >= cs) & (kc < cs + NA_WIN_COLS)
    col_idx = np.clip(kc - qc + NA_WIN_COLS - 1, 0, 2 * NA_WIN_COLS - 2)
    pick = (col_idx[:, :, None] == np.arange(2 * NA_WIN_COLS - 1)) & col_mask[:, :, None]
    tz = jnp.einsum("hrc,qkc->hqrk", rpb.astype(jnp.float32), jnp.asarray(pick, jnp.float32),
                    precision=lax.Precision.HIGHEST)
    tz = tz + jnp.asarray(np.where(col_mask, 0.0, MASK_VALUE)[None, :, None, :], jnp.float32)
    band = NA_WIN_ROWS * GRID_W
    variants = [tz[:, :, NA_WIN_ROWS - 1 - var: 2 * NA_WIN_ROWS - 1 - var, :]
                .reshape(NA_HEADS, GRID_W, band) for var in range(NA_WIN_ROWS)]
    return jnp.stack(variants, axis=0)


def _na_kernel(q_ref, k_ref, v_ref, z_ref, bias_ref, o_ref):
    scale = NA_HEAD_DIM ** -0.5
    lanes = 2 * NA_HEAD_DIM
    first = lax.broadcasted_iota(jnp.int32, (GRID_W, lanes), 1) < NA_HEAD_DIM
    for p in range(NA_HEADS // 2):
        sl = slice(p * lanes, (p + 1) * lanes)
        qp = q_ref[:, sl] * scale
        zero = jnp.zeros_like(qp)
        q_bd = jnp.concatenate([jnp.where(first, qp, zero), jnp.where(first, zero, qp)], axis=0)
        s = lax.dot_general(q_bd, k_ref[:, sl], (((1,), (1,)), ((), ())),
                            preferred_element_type=jnp.float32)
        s = s + bias_ref[0, p]
        m = jnp.max(s, axis=-1, keepdims=True)
        e = jnp.exp(s - m)
        l = jnp.sum(e, axis=-1, keepdims=True)
        o = jnp.dot(e.astype(jnp.bfloat16), v_ref[:, sl], preferred_element_type=jnp.float32)
        o = o / l
        o = jnp.where(first, o[:GRID_W], o[GRID_W:])
        z = z_ref[:, sl]
        o_ref[:, sl] = (o * (z * jax.nn.sigmoid(z))).astype(o_ref.dtype)


def _neighbourhood_attention(qkv, rest, bias_tab):
    t = qkv.shape[0]
    rows = t // GRID_W
    band = NA_WIN_ROWS * GRID_W

    def row_start(r):
        return jnp.clip(r - NA_WIN_ROWS // 2, 0, rows - NA_WIN_ROWS)

    return pl.pallas_call(
        _na_kernel,
        grid=(rows,),
        in_specs=[
            pl.BlockSpec((GRID_W, NA_WIDTH), lambda r: (r, 0)),
            pl.BlockSpec((pl.Element(band), pl.Element(NA_WIDTH)),
                         lambda r: (row_start(r) * GRID_W, NA_WIDTH)),
            pl.BlockSpec((pl.Element(band), pl.Element(NA_WIDTH)),
                         lambda r: (row_start(r) * GRID_W, 2 * NA_WIDTH)),
            pl.BlockSpec((GRID_W, NA_WIDTH), lambda r: (r, 0)),
            pl.BlockSpec((1, NA_HEADS // 2, 2 * GRID_W, band), lambda r: (r - row_start(r), 0, 0, 0)),
        ],
        out_specs=pl.BlockSpec((GRID_W, NA_WIDTH), lambda r: (r, 0)),
        out_shape=jax.ShapeDtypeStruct((t, NA_WIDTH), jnp.bfloat16),
        compiler_params=_params("arbitrary"),
        name="neighbourhood_attention",
    )(qkv, qkv, qkv, rest, bias_tab.reshape(NA_WIN_ROWS, NA_HEADS // 2, 2 * GRID_W, band))


HG_CHUNK = 64
HG_BLOCK = 512


SUBLANES = 8


def _hgrn_level_masks():
    c = HG_CHUNK
    t = np.arange(c)[:, None]
    s = np.arange(c)[None, :]
    out = []
    for reverse in (False, True):
        tt, ss = (c - 1 - t, c - 1 - s) if reverse else (t, s)
        x = tt ^ ss
        levels = [tt == ss]
        m = 1
        while m < c:
            levels.append((tt > ss) & (x >= m) & (x < 2 * m))
            m *= 2
        out.append(np.stack(levels))
    return np.stack(out).astype(np.float32)


def _hgrn_chunk(q, fl, v, lb, st, mask_ref, reverse):
    c, sub = HG_CHUNK, SUBLANES
    nv = c // sub
    bf16 = jnp.bfloat16
    f = lb + (1.0 - lb) * jax.nn.sigmoid(fl)
    g = jnp.log2(f)
    k = (1.0 - lb) * jax.nn.sigmoid(-fl)

    srow = lax.broadcasted_iota(jnp.int32, (sub, HG_DIM), 0)
    cp = (sub - 1 - srow) if reverse else srow

    def prev_shift(x, j):
        return pltpu.roll(x, (sub - j) if reverse else j, 0)

    def row_bcast(x, p):
        i = (sub - 1 - p) if reverse else p
        return jnp.broadcast_to(x[i:i + 1, :], (sub, HG_DIM))

    def rows(lst):
        return jnp.concatenate(lst[::-1] if reverse else lst, axis=0)

    blocks = range(nv - 1, -1, -1) if reverse else range(nv)
    gs = [g[b * sub:(b + 1) * sub] for b in blocks]
    loc = []
    for x in gs:
        j = 1
        while j < sub:
            x = x + jnp.where(cp >= j, prev_shift(x, j), 0.0)
            j *= 2
        loc.append(x)
    carry = [None, row_bcast(loc[0], sub - 1)]
    a = [loc[0]]
    for i in range(1, nv):
        a.append(loc[i] + carry[i])
        carry.append(carry[i] + row_bcast(loc[i], sub - 1))
    total = carry[nv]

    def level_exponents(m):
        if m >= sub:
            w = m // sub
            out = []
            for i in range(nv):
                mid = carry[(i // (2 * w)) * (2 * w) + w]
                out.append(a[i] - mid if (i // w) % 2 else mid - a[i])
            return out
        if m == 1:
            return [jnp.where((cp & 1) == 1, x, 0.0) for x in gs]
        out = []
        for x in a:
            r = row_bcast(x, m - 1)
            for blk in range(1, sub // (2 * m)):
                r = jnp.where(cp < blk * 2 * m, r, row_bcast(x, blk * 2 * m + m - 1))
            out.append(-jnp.abs(x - r))
        return out

    def qk(qm, km):
        return lax.dot_general(qm, km, (((1,), (1,)), ((), ())), preferred_element_type=jnp.float32)

    d = 1 if reverse else 0
    qb, kb = q.astype(bf16), k.astype(bf16)
    scores = mask_ref[d, 0] * qk(qb, kb)
    m, lvl = 1, 1
    while m < c:
        e = jnp.exp2(rows(level_exponents(m))).astype(bf16)
        scores = scores + mask_ref[d, lvl] * qk(qb * e, kb * e)
        m, lvl = 2 * m, lvl + 1

    q_in = (q * jnp.exp2(rows(a))).astype(bf16)
    k_out = (k * jnp.exp2(rows([total - x for x in a]))).astype(bf16)
    o = jnp.dot(scores.astype(bf16), v.astype(bf16), preferred_element_type=jnp.float32)
    o = o + lax.dot_general(q_in, st.astype(bf16), (((1,), (1,)), ((), ())),
                            preferred_element_type=jnp.float32)
    st_new = st * jnp.exp2(total[0:1, :]) + jnp.dot(v.T.astype(bf16), k_out,
                                                     preferred_element_type=jnp.float32)
    return o, st_new


def _hgrn_kernel(lbl_ref, mask_ref, qf_ref, ff_ref, vf_ref, qb_ref, fb_ref, vb_ref,
                 of_ref, ob_ref, st_ref, *, layer):
    @pl.when(pl.program_id(1) == 0)
    def _():
        st_ref[...] = jnp.zeros_like(st_ref)

    lg = lbl_ref[...]
    mx = jnp.max(lg, axis=0, keepdims=True)
    ex = jnp.exp(lg - mx)
    lb = jnp.sum(ex[: layer + 1], axis=0) / jnp.sum(ex, axis=0)
    lb_f, lb_b = lb[0:1, :], lb[1:2, :]

    n = HG_BLOCK // HG_CHUNK
    sf, sb = st_ref[0], st_ref[1]
    for ci in range(n):
        rf = slice(ci * HG_CHUNK, (ci + 1) * HG_CHUNK)
        rb = slice((n - 1 - ci) * HG_CHUNK, (n - ci) * HG_CHUNK)
        o_f, sf = _hgrn_chunk(qf_ref[rf, :], ff_ref[rf, :], vf_ref[rf, :], lb_f, sf, mask_ref, False)
        of_ref[rf, :] = o_f
        o_b, sb = _hgrn_chunk(qb_ref[rb, :], fb_ref[rb, :], vb_ref[rb, :], lb_b, sb, mask_ref, True)
        ob_ref[rb, :] = o_b
    st_ref[0] = sf
    st_ref[1] = sb


def _hgrn2(rest, lb_logits, layer):
    t = rest.shape[0]
    nb = t // HG_BLOCK
    layers = lb_logits.shape[0]
    masks = jnp.asarray(_hgrn_level_masks())
    col = lambda seg: seg * HG_HEADS
    q_c, ff_c, fb_c, i_c = col(1), col(2), col(3), col(4)
    blk = (HG_BLOCK, HG_DIM)
    fwd = lambda c0: pl.BlockSpec(blk, lambda h, b: (b, c0 + h))
    bwd = lambda c0: pl.BlockSpec(blk, lambda h, b: (nb - 1 - b, c0 + h))
    return pl.pallas_call(
        functools.partial(_hgrn_kernel, layer=layer),
        grid=(HG_HEADS, nb),
        in_specs=[pl.BlockSpec((layers, 2, HG_DIM), lambda h, b: (0, 0, h)),
                  pl.BlockSpec(masks.shape, lambda h, b: (0, 0, 0, 0)),
                  fwd(q_c), fwd(ff_c), fwd(i_c), bwd(q_c), bwd(fb_c), bwd(i_c)],
        out_specs=[pl.BlockSpec(blk, lambda h, b: (b, h)),
                   pl.BlockSpec(blk, lambda h, b: (nb - 1 - b, h))],
        out_shape=[jax.ShapeDtypeStruct((t, HG_WIDTH), jnp.float32)] * 2,
        scratch_shapes=[pltpu.VMEM((2, HG_DIM, HG_DIM), jnp.float32)],
        compiler_params=_params("parallel", "arbitrary"),
        name="hgrn2_bidirectional",
    )(lb_logits, masks, rest, rest, rest, rest, rest, rest)


def _merge_kernel(x_ref, oa_ref, of_ref, ob_ref, zb_ref, ga_ref, gb_ref, bga_ref, bgb_ref,
                  hgw_ref, wa_ref, wb_ref, wo_ref, pw_ref, o_ref):
    zb = zb_ref[...]
    gate_b = zb * jax.nn.sigmoid(zb)
    osum = of_ref[...] + ob_ref[...]
    parts = []
    for h in range(HG_HEADS):
        sl = slice(h * HG_DIM, (h + 1) * HG_DIM)
        oh = osum[:, sl]
        ms = jnp.mean(oh * oh, axis=-1, keepdims=True)
        parts.append(oh * lax.rsqrt(ms + NORM_EPS) * hgw_ref[:, sl])
    o_b = (jnp.concatenate(parts, axis=-1) * gate_b).astype(jnp.bfloat16)

    pa = jnp.dot(oa_ref[...], wa_ref[...], preferred_element_type=jnp.float32)
    pb = jnp.dot(o_b, wb_ref[...], preferred_element_type=jnp.float32)
    y = (jax.nn.sigmoid(ga_ref[...] + bga_ref[...]) * pa
         + jax.nn.sigmoid(gb_ref[...] + bgb_ref[...]) * pb)
    u = jnp.dot(y.astype(jnp.bfloat16), wo_ref[...], preferred_element_type=jnp.float32)
    ms = jnp.mean(u * u, axis=-1, keepdims=True)
    o_ref[...] = x_ref[...] + u * lax.rsqrt(ms + NORM_EPS) * pw_ref[...]


def _merge(x, oa, o_f, o_b, rest, b_gate, hg_norm_w, w_a, w_b, w_o, post_w, tm=256):
    t, d = x.shape
    seg = lambda s: s
    tile = lambda w, c: pl.BlockSpec((tm, w), lambda i: (i, c))
    const = lambda shape: pl.BlockSpec(shape, lambda i: (0, 0), pipeline_mode=pl.Buffered(1))
    ga_col = (NA_WIDTH + 5 * HG_WIDTH) // D_MODEL
    return pl.pallas_call(
        _merge_kernel,
        grid=(t // tm,),
        in_specs=[tile(d, 0), tile(NA_WIDTH, 0), tile(HG_WIDTH, 0), tile(HG_WIDTH, 0),
                  tile(HG_WIDTH, seg(5)), tile(d, ga_col), tile(d, ga_col + 1),
                  const((1, d)), const((1, d)), const((1, HG_WIDTH)),
                  const((NA_WIDTH, d)), const((HG_WIDTH, d)), const((d, d)), const((1, d))],
        out_specs=tile(d, 0),
        out_shape=jax.ShapeDtypeStruct((t, d), jnp.float32),
        compiler_params=_params("parallel"),
        name="merge_out_proj",
    )(x, oa, o_f, o_b, rest, rest, rest, b_gate[:d].reshape(1, d), b_gate[d:].reshape(1, d),
      hg_norm_w.reshape(1, HG_WIDTH), w_a, w_b, w_o, post_w.reshape(1, d))


def kernel(x, norm_pre_w, w_in, b_gate, na_rel_bias, hg_lb_logits, hg_norm_w, w_branch, w_out, norm_post_w):
    b, t, d = x.shape
    depth = w_in.shape[0]
    bf16 = jnp.bfloat16
    outs = []
    for bi in range(b):
        xb = x[bi]
        for l in range(depth):
            w_in_l = w_in[l].astype(bf16)
            xn = _rmsnorm(xb, norm_pre_w[l])
            qkv = _matmul(xn, w_in_l[:, :QKV_COLS], bf16, "in_proj_qkv")
            rest = _matmul(xn, w_in_l[:, QKV_COLS:], jnp.float32, "in_proj_rest")
            oa = _neighbourhood_attention(qkv, rest, _na_bias_table(na_rel_bias[l]))
            o_f, o_b = _hgrn2(rest, hg_lb_logits.astype(jnp.float32), l)
            wb = w_branch[l].astype(bf16)
            xb = _merge(xb, oa, o_f, o_b, rest, b_gate[l], hg_norm_w[l],
                        wb[:NA_WIDTH], wb[NA_WIDTH:], w_out[l].astype(bf16), norm_post_w[l])
        outs.append(xb)
    return jnp.stack(outs, axis=0)
```

```python
import functools

import jax
import jax.numpy as jnp
import numpy as np
from jax import lax
from jax.experimental import pallas as pl
from jax.experimental.pallas import tpu as pltpu

D_MODEL = 2048
GRID_W = 64
NA_HEAD_DIM = 64
NA_WIDTH = 1024
NA_HEADS = 16
NA_WIN_ROWS = 8
NA_WIN_COLS = 16
HG_DIM = 128
HG_HEADS = 8
HG_WIDTH = 1024
NORM_EPS = 1e-6
QKV_COLS = 3 * NA_WIDTH
REST_COLS = NA_WIDTH + 5 * HG_WIDTH + 2 * D_MODEL
MASK_VALUE = -1e30

VMEM_LIMIT_BYTES = 56 * 1024 * 1024


def _params(*sem, flags=None):
    return pltpu.CompilerParams(dimension_semantics=sem, vmem_limit_bytes=VMEM_LIMIT_BYTES, flags=flags)


def _rmsnorm_kernel(x_ref, w_ref, o_ref):
    x = x_ref[...]
    ms = jnp.mean(x * x, axis=-1, keepdims=True)
    o_ref[...] = (x * lax.rsqrt(ms + NORM_EPS) * w_ref[...]).astype(o_ref.dtype)


def _rmsnorm(x, w, tm=512):
    t, d = x.shape
    return pl.pallas_call(
        _rmsnorm_kernel,
        grid=(t // tm,),
        in_specs=[pl.BlockSpec((tm, d), lambda i: (i, 0)),
                  pl.BlockSpec((1, d), lambda i: (0, 0))],
        out_specs=pl.BlockSpec((tm, d), lambda i: (i, 0)),
        out_shape=jax.ShapeDtypeStruct((t, d), jnp.bfloat16),
        compiler_params=_params("parallel"),
        name="rmsnorm_pre",
    )(x, w.reshape(1, d))


def _in_proj_kernel(a_ref, w_ref, o_ref, wb_ref):
    @pl.when(pl.program_id(1) == 0)
    def _():
        wb_ref[...] = w_ref[...].astype(wb_ref.dtype)

    o_ref[...] = jnp.dot(a_ref[...], wb_ref[...],
                         preferred_element_type=jnp.float32).astype(o_ref.dtype)


def _in_proj(a, w, col0, ncols, out_dtype, name, tm=1024, tn=1024):
    m, k = a.shape
    assert col0 % tn == 0 and ncols % tn == 0 and m % tm == 0
    j0 = col0 // tn
    return pl.pallas_call(
        _in_proj_kernel,
        grid=(ncols // tn, m // tm),
        in_specs=[pl.BlockSpec((tm, k), lambda j, i: (i, 0)),
                  pl.BlockSpec((k, tn), lambda j, i: (0, j0 + j))],
        out_specs=pl.BlockSpec((tm, tn), lambda j, i: (i, j)),
        out_shape=jax.ShapeDtypeStruct((m, ncols), out_dtype),
        scratch_shapes=[pltpu.VMEM((k, tn), a.dtype)],
        compiler_params=_params("arbitrary", "arbitrary"),
        name=name,
    )(a, w)


def _na_bias_table(rpb):
    qc = np.arange(GRID_W)[:, None]
    kc = np.arange(GRID_W)[None, :]
    cs = np.clip(qc - NA_WIN_COLS // 2, 0, GRID_W - NA_WIN_COLS)
    col_mask = (kc >= cs) & (kc < cs + NA_WIN_COLS)
    col_idx = np.clip(kc - qc + NA_WIN_COLS - 1, 0, 2 * NA_WIN_COLS - 2)
    pick = (col_idx[:, :, None] == np.arange(2 * NA_WIN_COLS - 1)) & col_mask[:, :, None]
    tz = jnp.einsum("hrc,qkc->hqrk", rpb.astype(jnp.float32), jnp.asarray(pick, jnp.float32),
                    precision=lax.Precision.HIGHEST)
    tz = tz + jnp.asarray(np.where(col_mask, 0.0, MASK_VALUE)[None, :, None, :], jnp.float32)
    band = NA_WIN_ROWS * GRID_W
    variants = [tz[:, :, NA_WIN_ROWS - 1 - var: 2 * NA_WIN_ROWS - 1 - var, :]
                .reshape(NA_HEADS, GRID_W, band) for var in range(NA_WIN_ROWS)]
    return jnp.stack(variants, axis=0)


def _na_kernel(q_ref, k_ref, v_ref, z_ref, bias_ref, o_ref, s_ref, m_ref):
    scale = NA_HEAD_DIM ** -0.5
    lanes = 2 * NA_HEAD_DIM
    first = lax.broadcasted_iota(jnp.int32, (GRID_W, lanes), 1) < NA_HEAD_DIM
    pairs = NA_HEADS // 2

    def scores(p):
        sl = slice(p * lanes, (p + 1) * lanes)
        qp = q_ref[:, sl] * scale
        zero = jnp.zeros_like(qp)
        q_bd = jnp.concatenate([jnp.where(first, qp, zero), jnp.where(first, zero, qp)], axis=0)
        return lax.dot_general(q_bd, k_ref[:, sl], (((1,), (1,)), ((), ())),
                               preferred_element_type=jnp.float32)

    for p in range(pairs):
        s = scores(p) + bias_ref[0, p]
        s_ref[p] = s
        m_ref[p] = jnp.broadcast_to(jnp.max(s, axis=-1, keepdims=True), (2 * GRID_W, lanes))
    for p in range(pairs):
        sl = slice(p * lanes, (p + 1) * lanes)
        m = m_ref[p]
        e = jnp.exp(s_ref[p] - jnp.concatenate([m] * (s_ref.shape[-1] // lanes), axis=-1))
        l = jnp.sum(e, axis=-1, keepdims=True)
        o = jnp.dot(e.astype(jnp.bfloat16), v_ref[:, sl], preferred_element_type=jnp.float32)
        o = o / l
        o = jnp.where(first, o[:GRID_W], o[GRID_W:])
        z = z_ref[:, sl]
        o_ref[:, sl] = (o * (z * jax.nn.sigmoid(z))).astype(o_ref.dtype)


def _neighbourhood_attention(qkv, rest, bias_tab):
    t = qkv.shape[0]
    rows = t // GRID_W
    band = NA_WIN_ROWS * GRID_W

    def row_start(r):
        return jnp.clip(r - NA_WIN_ROWS // 2, 0, rows - NA_WIN_ROWS)

    return pl.pallas_call(
        _na_kernel,
        grid=(rows,),
        in_specs=[
            pl.BlockSpec((GRID_W, NA_WIDTH), lambda r: (r, 0)),
            pl.BlockSpec((pl.Element(band), pl.Element(NA_WIDTH)),
                         lambda r: (row_start(r) * GRID_W, NA_WIDTH)),
            pl.BlockSpec((pl.Element(band), pl.Element(NA_WIDTH)),
                         lambda r: (row_start(r) * GRID_W, 2 * NA_WIDTH)),
            pl.BlockSpec((GRID_W, NA_WIDTH), lambda r: (r, 0)),
            pl.BlockSpec((1, NA_HEADS // 2, 2 * GRID_W, band), lambda r: (r - row_start(r), 0, 0, 0)),
        ],
        out_specs=pl.BlockSpec((GRID_W, NA_WIDTH), lambda r: (r, 0)),
        out_shape=jax.ShapeDtypeStruct((t, NA_WIDTH), jnp.bfloat16),
        scratch_shapes=[pltpu.VMEM((NA_HEADS // 2, 2 * GRID_W, band), jnp.float32),
                        pltpu.VMEM((NA_HEADS // 2, 2 * GRID_W, 2 * NA_HEAD_DIM), jnp.float32)],
        compiler_params=_params("arbitrary"),
        name="neighbourhood_attention",
    )(qkv, qkv, qkv, rest, bias_tab.reshape(NA_WIN_ROWS, NA_HEADS // 2, 2 * GRID_W, band))


HG_CHUNK = 64
HG_BLOCK = 512


SUBLANES = 8


def _hgrn_level_masks():
    c = HG_CHUNK
    t = np.arange(c)[:, None]
    s = np.arange(c)[None, :]
    out = []
    for reverse in (False, True):
        tt, ss = (c - 1 - t, c - 1 - s) if reverse else (t, s)
        x = tt ^ ss
        levels = [tt == ss]
        m = 1
        while m < c:
            levels.append((tt > ss) & (x >= m) & (x < 2 * m))
            m *= 2
        out.append(np.stack(levels))
    return np.stack(out).astype(np.float32)


def _hgrn_chunk(q, fl, v, lb, mask_ref, reverse):
    c, sub = HG_CHUNK, SUBLANES
    nv = c // sub
    bf16 = jnp.bfloat16
    f = lb + (1.0 - lb) * jax.nn.sigmoid(fl)
    g = jnp.log2(f)
    k = (1.0 - lb) * jax.nn.sigmoid(-fl)

    srow = lax.broadcasted_iota(jnp.int32, (sub, HG_DIM), 0)
    cp = (sub - 1 - srow) if reverse else srow

    def prev_shift(x, j):
        return pltpu.roll(x, (sub - j) if reverse else j, 0)

    def row_bcast(x, p):
        i = (sub - 1 - p) if reverse else p
        return jnp.broadcast_to(x[i:i + 1, :], (sub, HG_DIM))

    def rows(lst):
        return jnp.concatenate(lst[::-1] if reverse else lst, axis=0)

    blocks = range(nv - 1, -1, -1) if reverse else range(nv)
    gs = [g[b * sub:(b + 1) * sub] for b in blocks]
    loc = []
    for x in gs:
        j = 1
        while j < sub:
            x = x + jnp.where(cp >= j, prev_shift(x, j), 0.0)
            j *= 2
        loc.append(x)
    carry = [None, row_bcast(loc[0], sub - 1)]
    a = [loc[0]]
    for i in range(1, nv):
        a.append(loc[i] + carry[i])
        carry.append(carry[i] + row_bcast(loc[i], sub - 1))
    total = carry[nv]

    def level_exponents(m):
        if m >= sub:
            w = m // sub
            out = []
            for i in range(nv):
                mid = carry[(i // (2 * w)) * (2 * w) + w]
                out.append(a[i] - mid if (i // w) % 2 else mid - a[i])
            return out
        if m == 1:
            return [jnp.where((cp & 1) == 1, x, 0.0) for x in gs]
        out = []
        for x in a:
            r = row_bcast(x, m - 1)
            for blk in range(1, sub // (2 * m)):
                r = jnp.where(cp < blk * 2 * m, r, row_bcast(x, blk * 2 * m + m - 1))
            out.append(-jnp.abs(x - r))
        return out

    def qk(qm, km):
        return lax.dot_general(qm, km, (((1,), (1,)), ((), ())), preferred_element_type=jnp.float32)

    d = 1 if reverse else 0
    qb, kb = q.astype(bf16), k.astype(bf16)
    scores = mask_ref[d, 0] * qk(qb, kb)
    m, lvl = 1, 1
    while m < c:
        e = jnp.exp2(rows(level_exponents(m))).astype(bf16)
        scores = scores + mask_ref[d, lvl] * qk(qb * e, kb * e)
        m, lvl = 2 * m, lvl + 1

    q_in = (q * jnp.exp2(rows(a))).astype(bf16)
    k_out = (k * jnp.exp2(rows([total - x for x in a]))).astype(bf16)
    return scores.astype(bf16), q_in, k_out, jnp.exp2(total[0:1, :]), v.astype(bf16), v.T.astype(bf16)


def _hgrn_chunk_state(intra, st):
    scores, q_in, k_out, decay, vb, vtb = intra
    o = jnp.dot(scores, vb, preferred_element_type=jnp.float32)
    o = o + lax.dot_general(q_in, st.astype(jnp.bfloat16), (((1,), (1,)), ((), ())),
                            preferred_element_type=jnp.float32)
    st_new = st * decay + jnp.dot(vtb, k_out, preferred_element_type=jnp.float32)
    return o, st_new


def _hgrn_kernel(lbl_ref, mask_ref, qf_ref, ff_ref, vf_ref, qb_ref, fb_ref, vb_ref,
                 of_ref, ob_ref, st_ref, *, layer):
    @pl.when(pl.program_id(1) == 0)
    def _():
        st_ref[...] = jnp.zeros_like(st_ref)

    lg = lbl_ref[...]
    mx = jnp.max(lg, axis=0, keepdims=True)
    ex = jnp.exp(lg - mx)
    lb = jnp.sum(ex[: layer + 1], axis=0) / jnp.sum(ex, axis=0)
    lb_f, lb_b = lb[0:1, :], lb[1:2, :]

    n = HG_BLOCK // HG_CHUNK
    state = [st_ref[0], st_ref[1]]
    refs = ((qf_ref, ff_ref, vf_ref, of_ref, lb_f), (qb_ref, fb_ref, vb_ref, ob_ref, lb_b))
    pending = None

    def finish(job):
        d, rows, intra = job
        o, state[d] = _hgrn_chunk_state(intra, state[d])
        refs[d][3][rows, :] = o

    for ci in range(n):
        for d in (0, 1):
            blk = ci if d == 0 else n - 1 - ci
            rows = slice(blk * HG_CHUNK, (blk + 1) * HG_CHUNK)
            q_ref, f_ref, v_ref, _, lb_d = refs[d]
            intra = _hgrn_chunk(q_ref[rows, :], f_ref[rows, :], v_ref[rows, :], lb_d, mask_ref, d == 1)
            if pending is not None:
                finish(pending)
            pending = (d, rows, intra)
    finish(pending)
    st_ref[0] = state[0]
    st_ref[1] = state[1]


def _hgrn2(rest, lb_logits, layer):
    t = rest.shape[0]
    nb = t // HG_BLOCK
    layers = lb_logits.shape[0]
    masks = jnp.asarray(_hgrn_level_masks())
    col = lambda seg: seg * HG_HEADS
    q_c, ff_c, fb_c, i_c = col(1), col(2), col(3), col(4)
    blk = (HG_BLOCK, HG_DIM)
    fwd = lambda c0: pl.BlockSpec(blk, lambda h, b: (b, c0 + h))
    bwd = lambda c0: pl.BlockSpec(blk, lambda h, b: (nb - 1 - b, c0 + h))
    return pl.pallas_call(
        functools.partial(_hgrn_kernel, layer=layer),
        grid=(HG_HEADS, nb),
        in_specs=[pl.BlockSpec((layers, 2, HG_DIM), lambda h, b: (0, 0, h)),
                  pl.BlockSpec(masks.shape, lambda h, b: (0, 0, 0, 0)),
                  fwd(q_c), fwd(ff_c), fwd(i_c), bwd(q_c), bwd(fb_c), bwd(i_c)],
        out_specs=[pl.BlockSpec(blk, lambda h, b: (b, h)),
                   pl.BlockSpec(blk, lambda h, b: (nb - 1 - b, h))],
        out_shape=[jax.ShapeDtypeStruct((t, HG_WIDTH), jnp.float32)] * 2,
        scratch_shapes=[pltpu.VMEM((2, HG_DIM, HG_DIM), jnp.float32)],
        compiler_params=_params("parallel", "arbitrary"),
        name="hgrn2_bidirectional",
    )(lb_logits, masks, rest, rest, rest, rest, rest, rest)


def _merge_kernel(x_ref, oa_ref, of_ref, ob_ref, zb_ref, ga_ref, gb_ref, bga_ref, bgb_ref,
                  hgw_ref, wa_ref, wb_ref, wo_ref, pw_ref, o_ref):
    zb = zb_ref[...]
    gate_b = zb * jax.nn.sigmoid(zb)
    osum = of_ref[...] + ob_ref[...]
    parts = []
    for h in range(HG_HEADS):
        sl = slice(h * HG_DIM, (h + 1) * HG_DIM)
        oh = osum[:, sl]
        ms = jnp.mean(oh * oh, axis=-1, keepdims=True)
        parts.append(oh * lax.rsqrt(ms + NORM_EPS) * hgw_ref[:, sl])
    o_b = (jnp.concatenate(parts, axis=-1) * gate_b).astype(jnp.bfloat16)

    pa = jnp.dot(oa_ref[...], wa_ref[...], preferred_element_type=jnp.float32)
    pb = jnp.dot(o_b, wb_ref[...], preferred_element_type=jnp.float32)
    y = (jax.nn.sigmoid(ga_ref[...] + bga_ref[...]) * pa
         + jax.nn.sigmoid(gb_ref[...] + bgb_ref[...]) * pb)
    u = jnp.dot(y.astype(jnp.bfloat16), wo_ref[...], preferred_element_type=jnp.float32)
    ms = jnp.mean(u * u, axis=-1, keepdims=True)
    o_ref[...] = x_ref[...] + u * lax.rsqrt(ms + NORM_EPS) * pw_ref[...]


def _merge(x, oa, o_f, o_b, rest, b_gate, hg_norm_w, w_br, w_o, post_w, tm=256):
    t, d = x.shape
    tile = lambda w, c: pl.BlockSpec((tm, w), lambda i: (i, c))
    const = lambda shape, r=0, c=0: pl.BlockSpec(shape, lambda i: (r, c), pipeline_mode=pl.Buffered(1))
    zb_col = 5
    ga_col = (NA_WIDTH + 5 * HG_WIDTH) // D_MODEL
    bg = b_gate.reshape(1, 2 * d)
    return pl.pallas_call(
        _merge_kernel,
        grid=(t // tm,),
        in_specs=[tile(d, 0), tile(NA_WIDTH, 0), tile(HG_WIDTH, 0), tile(HG_WIDTH, 0),
                  tile(HG_WIDTH, zb_col), tile(d, ga_col), tile(d, ga_col + 1),
                  const((1, d)), const((1, d), 0, 1), const((1, HG_WIDTH)),
                  const((NA_WIDTH, d)), const((HG_WIDTH, d), NA_WIDTH // HG_WIDTH, 0),
                  const((d, d)), const((1, d))],
        out_specs=tile(d, 0),
        out_shape=jax.ShapeDtypeStruct((t, d), jnp.float32),
        compiler_params=_params("parallel"),
        name="merge_out_proj",
    )(x, oa, o_f, o_b, rest, rest, rest, bg, bg,
      hg_norm_w.reshape(1, HG_WIDTH), w_br, w_br, w_o, post_w.reshape(1, d))


def kernel(x, norm_pre_w, w_in, b_gate, na_rel_bias, hg_lb_logits, hg_norm_w, w_branch, w_out, norm_post_w):
    b, t, d = x.shape
    depth = w_in.shape[0]
    bf16 = jnp.bfloat16
    outs = []
    for bi in range(b):
        xb = x[bi]
        for l in range(depth):
            xn = _rmsnorm(xb, norm_pre_w[l])
            qkv = _in_proj(xn, w_in[l], 0, QKV_COLS, bf16, "in_proj_qkv")
            rest = _in_proj(xn, w_in[l], QKV_COLS, REST_COLS, jnp.float32, "in_proj_rest")
            oa = _neighbourhood_attention(qkv, rest, _na_bias_table(na_rel_bias[l]))
            o_f, o_b = _hgrn2(rest, hg_lb_logits.astype(jnp.float32), l)
            xb = _merge(xb, oa, o_f, o_b, rest, b_gate[l], hg_norm_w[l],
                        w_branch[l].astype(bf16), w_out[l].astype(bf16), norm_post_w[l])
        outs.append(xb)
    return outs[0][None] if b == 1 else jnp.stack(outs, axis=0)
```

```python
import functools

import jax
import jax.numpy as jnp
import numpy as np
from jax import lax
from jax.experimental import pallas as pl
from jax.experimental.pallas import tpu as pltpu

D_MODEL = 2048
GRID_W = 64
NA_HEAD_DIM = 64
NA_WIDTH = 1024
NA_HEADS = 16
NA_WIN_ROWS = 8
NA_WIN_COLS = 16
HG_DIM = 128
HG_HEADS = 8
HG_WIDTH = 1024
NORM_EPS = 1e-6
QKV_COLS = 3 * NA_WIDTH
REST_COLS = NA_WIDTH + 5 * HG_WIDTH + 2 * D_MODEL
MASK_VALUE = -1e30
LOG2_E = 1.4426950408889634

VMEM_LIMIT_BYTES = 56 * 1024 * 1024


def _params(*sem, flags=None):
    return pltpu.CompilerParams(dimension_semantics=sem, vmem_limit_bytes=VMEM_LIMIT_BYTES, flags=flags)


def _rmsnorm_kernel(x_ref, w_ref, o_ref):
    x = x_ref[...]
    ms = jnp.mean(x * x, axis=-1, keepdims=True)
    o_ref[...] = (x * lax.rsqrt(ms + NORM_EPS) * w_ref[...]).astype(o_ref.dtype)


def _rmsnorm(x, w, tm=1024):
    t, d = x.shape
    return pl.pallas_call(
        _rmsnorm_kernel,
        grid=(t // tm,),
        in_specs=[pl.BlockSpec((tm, d), lambda i: (i, 0)),
                  pl.BlockSpec((1, d), lambda i: (0, 0))],
        out_specs=pl.BlockSpec((tm, d), lambda i: (i, 0)),
        out_shape=jax.ShapeDtypeStruct((t, d), jnp.bfloat16),
        compiler_params=_params("parallel"),
        name="rmsnorm_pre",
    )(x, w.reshape(1, d))


def _in_proj_kernel(a_ref, w_ref, o_ref, wb_ref, *, first_block_scale):
    @pl.when(pl.program_id(1) == 0)
    def _():
        wb_ref[...] = w_ref[...].astype(wb_ref.dtype)

    acc = jnp.dot(a_ref[...], wb_ref[...], preferred_element_type=jnp.float32)
    if first_block_scale is not None:
        acc = acc * jnp.where(pl.program_id(0) == 0, first_block_scale, 1.0)
    o_ref[...] = acc.astype(o_ref.dtype)


def _in_proj(a, w, col0, ncols, out_dtype, name, first_block_scale=None, tm=1024, tn=1024):
    m, k = a.shape
    assert col0 % tn == 0 and ncols % tn == 0 and m % tm == 0
    j0 = col0 // tn
    return pl.pallas_call(
        functools.partial(_in_proj_kernel, first_block_scale=first_block_scale),
        grid=(ncols // tn, m // tm),
        in_specs=[pl.BlockSpec((tm, k), lambda j, i: (i, 0)),
                  pl.BlockSpec((k, tn), lambda j, i: (0, j0 + j))],
        out_specs=pl.BlockSpec((tm, tn), lambda j, i: (i, j)),
        out_shape=jax.ShapeDtypeStruct((m, ncols), out_dtype),
        scratch_shapes=[pltpu.VMEM((k, tn), a.dtype)],
        compiler_params=_params("arbitrary", "arbitrary"),
        name=name,
    )(a, w)


def _na_bias_table(rpb):
    qc = np.arange(GRID_W)[:, None]
    kc = np.arange(GRID_W)[None, :]
    cs = np.clip(qc - NA_WIN_COLS // 2, 0, GRID_W - NA_WIN_COLS)
    col_mask = (kc >= cs) & (kc < cs + NA_WIN_COLS)
    col_idx = np.clip(kc - qc + NA_WIN_COLS - 1, 0, 2 * NA_WIN_COLS - 2)
    pick = (col_idx[:, :, None] == np.arange(2 * NA_WIN_COLS - 1)) & col_mask[:, :, None]
    tz = jnp.einsum("hrc,qkc->hqrk", rpb.astype(jnp.float32) * LOG2_E, jnp.asarray(pick, jnp.float32),
                    precision=lax.Precision.HIGHEST)
    tz = tz + jnp.asarray(np.where(col_mask, 0.0, MASK_VALUE)[None, :, None, :], jnp.float32)
    band = NA_WIN_ROWS * GRID_W
    variants = [tz[:, :, NA_WIN_ROWS - 1 - var: 2 * NA_WIN_ROWS - 1 - var, :]
                .reshape(NA_HEADS, GRID_W, band) for var in range(NA_WIN_ROWS)]
    return jnp.stack(variants, axis=0)


def _na_kernel(q_ref, k0_ref, v0_ref, kn_ref, vn_ref, z_ref, bias_ref, o_ref,
               k_ring, v_ring, s_ref, m_ref, *, rows):
    lanes = 2 * NA_HEAD_DIM
    band = NA_WIN_ROWS * GRID_W
    half = NA_WIN_ROWS // 2
    first = lax.broadcasted_iota(jnp.int32, (GRID_W, lanes), 1) < NA_HEAD_DIM
    pairs = NA_HEADS // 2
    ones = jnp.ones((band, lanes), jnp.bfloat16)

    r = pl.program_id(0)

    @pl.when(r == 0)
    def _():
        k_ring[...] = k0_ref[...]
        v_ring[...] = v0_ref[...]

    @pl.when((r > half) & (r <= rows - half))
    def _():
        slot = (r + half - 1) & (NA_WIN_ROWS - 1)
        k_ring[slot] = kn_ref[0]
        v_ring[slot] = vn_ref[0]

    start = jnp.clip(r - half, 0, rows - NA_WIN_ROWS)
    slots = [(start + j) & (NA_WIN_ROWS - 1) for j in range(NA_WIN_ROWS)]

    for p in range(pairs):
        sl = slice(p * lanes, (p + 1) * lanes)
        qp = q_ref[:, sl]
        zero = jnp.zeros_like(qp)
        q_bd = jnp.concatenate([jnp.where(first, qp, zero), jnp.where(first, zero, qp)], axis=0)
        k_band = jnp.concatenate([k_ring[j, :, sl] for j in slots], axis=0)
        s = lax.dot_general(q_bd, k_band, (((1,), (1,)), ((), ())),
                            preferred_element_type=jnp.float32) + bias_ref[0, p]
        s_ref[p] = s
        m_ref[p] = jnp.broadcast_to(jnp.max(s, axis=-1, keepdims=True), (2 * GRID_W, lanes))
    for p in range(pairs):
        sl = slice(p * lanes, (p + 1) * lanes)
        m = m_ref[p]
        e = jnp.exp2(s_ref[p] - jnp.concatenate([m] * (band // lanes), axis=-1))
        v_band = jnp.concatenate([v_ring[j, :, sl] for j in slots], axis=0)
        v_ext = jnp.concatenate([v_band, ones], axis=1)
        ol = jnp.dot(e.astype(jnp.bfloat16), v_ext, preferred_element_type=jnp.float32)
        o = ol[:, :lanes] / ol[:, lanes:]
        o = jnp.where(first, o[:GRID_W], o[GRID_W:])
        z = z_ref[:, sl]
        o_ref[:, sl] = (o * (z * jax.nn.sigmoid(z))).astype(o_ref.dtype)


def _neighbourhood_attention(qkv, rest, bias_tab):
    t = qkv.shape[0]
    rows = t // GRID_W
    band = NA_WIN_ROWS * GRID_W

    def row_start(r):
        return jnp.clip(r - NA_WIN_ROWS // 2, 0, rows - NA_WIN_ROWS)

    row_blk = (GRID_W, NA_WIDTH)
    ring = (NA_WIN_ROWS, GRID_W, NA_WIDTH)
    half = NA_WIN_ROWS // 2
    first_rows = [pl.BlockSpec(ring, lambda r, c=c: (0, 0, c), pipeline_mode=pl.Buffered(1)) for c in (1, 2)]
    new_row = [pl.BlockSpec((1,) + row_blk,
                            lambda r, c=c: (jnp.clip(r + half - 1, NA_WIN_ROWS - 1, rows - 1), 0, c))
               for c in (1, 2)]
    qkv3 = qkv.reshape(rows, GRID_W, QKV_COLS)
    return pl.pallas_call(
        functools.partial(_na_kernel, rows=rows),
        grid=(rows,),
        in_specs=[pl.BlockSpec(row_blk, lambda r: (r, 0))] + first_rows + new_row + [
            pl.BlockSpec(row_blk, lambda r: (r, 0)),
            pl.BlockSpec((1, NA_HEADS // 2, 2 * GRID_W, band), lambda r: (r - row_start(r), 0, 0, 0)),
        ],
        out_specs=pl.BlockSpec(row_blk, lambda r: (r, 0)),
        out_shape=jax.ShapeDtypeStruct((t, NA_WIDTH), jnp.bfloat16),
        scratch_shapes=[pltpu.VMEM(ring, qkv.dtype), pltpu.VMEM(ring, qkv.dtype),
                        pltpu.VMEM((NA_HEADS // 2, 2 * GRID_W, band), jnp.float32),
                        pltpu.VMEM((NA_HEADS // 2, 2 * GRID_W, 2 * NA_HEAD_DIM), jnp.float32)],
        compiler_params=_params("arbitrary"),
        name="neighbourhood_attention",
    )(qkv, qkv3, qkv3, qkv3, qkv3, rest,
      bias_tab.reshape(NA_WIN_ROWS, NA_HEADS // 2, 2 * GRID_W, band))


HG_CHUNK = 64
HG_BLOCK = 2048


SUBLANES = 8


def _hgrn_level_masks():
    c = HG_CHUNK
    t = np.arange(c)[:, None]
    s = np.arange(c)[None, :]
    out = []
    for reverse in (False, True):
        tt, ss = (c - 1 - t, c - 1 - s) if reverse else (t, s)
        x = tt ^ ss
        levels = [tt == ss]
        m = 1
        while m < c:
            levels.append((tt > ss) & (x >= m) & (x < 2 * m))
            m *= 2
        if len(levels) % 2:
            levels.append(np.zeros((c, c), bool))
        out.append(np.stack([np.concatenate(levels[i:i + 2], axis=1) for i in range(0, len(levels), 2)]))
    return np.stack(out).astype(np.float32)


def _hgrn_chunk(q, fl, v, lb, mask_ref, reverse):
    c, sub = HG_CHUNK, SUBLANES
    nv = c // sub
    bf16 = jnp.bfloat16
    en = jnp.exp2(jnp.abs(fl) * (-LOG2_E))
    big = 1.0 / (1.0 + en)
    small = en * big
    nonneg = fl >= 0.0
    f = lb + (1.0 - lb) * jnp.where(nonneg, big, small)
    g = jnp.log2(f)
    k = (1.0 - lb) * jnp.where(nonneg, small, big)

    srow = lax.broadcasted_iota(jnp.int32, (sub, HG_DIM), 0)
    cp = (sub - 1 - srow) if reverse else srow

    def prev_shift(x, j):
        return pltpu.roll(x, (sub - j) if reverse else j, 0)

    def row_bcast(x, p):
        i = (sub - 1 - p) if reverse else p
        return jnp.broadcast_to(x[i:i + 1, :], (sub, HG_DIM))

    def rows(lst):
        return jnp.concatenate(lst[::-1] if reverse else lst, axis=0)

    blocks = range(nv - 1, -1, -1) if reverse else range(nv)
    gs = [g[b * sub:(b + 1) * sub] for b in blocks]
    loc = []
    for x in gs:
        j = 1
        while j < sub:
            x = x + jnp.where(cp >= j, prev_shift(x, j), 0.0)
            j *= 2
        loc.append(x)
    carry = [None, row_bcast(loc[0], sub - 1)]
    a = [loc[0]]
    for i in range(1, nv):
        a.append(loc[i] + carry[i])
        carry.append(carry[i] + row_bcast(loc[i], sub - 1))
    total = carry[nv]

    def level_exponents(m):
        if m >= sub:
            w = m // sub
            out = []
            for i in range(nv):
                mid = carry[(i // (2 * w)) * (2 * w) + w]
                out.append(a[i] - mid if (i // w) % 2 else mid - a[i])
            return out
        if m == 1:
            return [jnp.where((cp & 1) == 1, x, 0.0) for x in gs]
        out = []
        for x in a:
            r = row_bcast(x, m - 1)
            for blk in range(1, sub // (2 * m)):
                r = jnp.where(cp < blk * 2 * m, r, row_bcast(x, blk * 2 * m + m - 1))
            out.append(-jnp.abs(x - r))
        return out

    d = 1 if reverse else 0
    qb, kb = q.astype(bf16), k.astype(bf16)
    groups = [(qb, kb)]
    m = 1
    while m < c:
        e = jnp.exp2(rows(level_exponents(m))).astype(bf16)
        groups.append((qb * e, kb * e))
        m *= 2

    zeros = jnp.zeros((c, HG_DIM), bf16)
    scores = None
    for slab in range(0, len(groups), 2):
        qa, ka = groups[slab]
        if slab + 1 < len(groups):
            qn, kn = groups[slab + 1]
            lhs = jnp.concatenate([qa, qn], axis=1)
            rhs = jnp.concatenate([jnp.concatenate([ka, zeros], axis=1),
                                   jnp.concatenate([zeros, kn], axis=1)], axis=0)
        else:
            lhs, rhs = qa, jnp.concatenate([ka, zeros], axis=0)
        part = mask_ref[d, slab // 2] * lax.dot_general(lhs, rhs, (((1,), (1,)), ((), ())),
                                                         preferred_element_type=jnp.float32)
        scores = part if scores is None else scores + part

    q_in = (q * jnp.exp2(rows(a))).astype(bf16)
    k_out = (k * jnp.exp2(rows([total - x for x in a]))).astype(bf16)
    vb = v.astype(bf16)
    return (scores.astype(bf16), q_in, k_out, jnp.exp2(total[0:1, :]),
            jnp.concatenate([vb, vb], axis=0), v.T.astype(bf16))


def _hgrn_chunk_state(intra, st):
    scores, q_in, k_out, decay, vv, vtb = intra
    o = jnp.dot(scores, vv, preferred_element_type=jnp.float32)
    o = o + lax.dot_general(q_in, st.astype(jnp.bfloat16), (((1,), (1,)), ((), ())),
                            preferred_element_type=jnp.float32)
    st_new = st * decay + jnp.dot(vtb, k_out, preferred_element_type=jnp.float32)
    return o, st_new


def _hgrn_kernel(lbl_ref, mask_ref, qf_ref, ff_ref, vf_ref, qb_ref, fb_ref, vb_ref,
                 of_ref, ob_ref, st_ref, *, layer):
    @pl.when(pl.program_id(1) == 0)
    def _():
        st_ref[...] = jnp.zeros_like(st_ref)

    lg = lbl_ref[...]
    mx = jnp.max(lg, axis=0, keepdims=True)
    ex = jnp.exp(lg - mx)
    lb = jnp.sum(ex[: layer + 1], axis=0) / jnp.sum(ex, axis=0)
    lb_f, lb_b = lb[0:1, :], lb[1:2, :]

    n = HG_BLOCK // HG_CHUNK
    state = [st_ref[0], st_ref[1]]
    refs = ((qf_ref, ff_ref, vf_ref, of_ref, lb_f), (qb_ref, fb_ref, vb_ref, ob_ref, lb_b))
    pending = None

    def finish(job):
        d, rows, intra = job
        o, state[d] = _hgrn_chunk_state(intra, state[d])
        refs[d][3][rows, :] = o

    for ci in range(n):
        for d in (0, 1):
            blk = ci if d == 0 else n - 1 - ci
            rows = slice(blk * HG_CHUNK, (blk + 1) * HG_CHUNK)
            q_ref, f_ref, v_ref, _, lb_d = refs[d]
            intra = _hgrn_chunk(q_ref[rows, :], f_ref[rows, :], v_ref[rows, :], lb_d, mask_ref, d == 1)
            if pending is not None:
                finish(pending)
            pending = (d, rows, intra)
    finish(pending)
    st_ref[0] = state[0]
    st_ref[1] = state[1]


def _hgrn2(rest, lb_logits, layer):
    t = rest.shape[0]
    nb = t // HG_BLOCK
    layers = lb_logits.shape[0]
    masks = jnp.asarray(_hgrn_level_masks())
    col = lambda seg: seg * HG_HEADS
    q_c, ff_c, fb_c, i_c = col(1), col(2), col(3), col(4)
    blk = (HG_BLOCK, HG_DIM)
    fwd = lambda c0: pl.BlockSpec(blk, lambda h, b: (b, c0 + h))
    bwd = lambda c0: pl.BlockSpec(blk, lambda h, b: (nb - 1 - b, c0 + h))
    return pl.pallas_call(
        functools.partial(_hgrn_kernel, layer=layer),
        grid=(HG_HEADS, nb),
        in_specs=[pl.BlockSpec((layers, 2, HG_DIM), lambda h, b: (0, 0, h)),
                  pl.BlockSpec(masks.shape, lambda h, b: (0, 0, 0, 0)),
                  fwd(q_c), fwd(ff_c), fwd(i_c), bwd(q_c), bwd(fb_c), bwd(i_c)],
        out_specs=[pl.BlockSpec(blk, lambda h, b: (b, h)),
                   pl.BlockSpec(blk, lambda h, b: (nb - 1 - b, h))],
        out_shape=[jax.ShapeDtypeStruct((t, HG_WIDTH), jnp.float32)] * 2,
        scratch_shapes=[pltpu.VMEM((2, HG_DIM, HG_DIM), jnp.float32)],
        compiler_params=_params("parallel", "arbitrary"),
        name="hgrn2_bidirectional",
    )(lb_logits, masks, rest, rest, rest, rest, rest, rest)


def _merge_kernel(oa_ref, of_ref, ob_ref, zb_ref, ga_ref, gb_ref, bga_ref, bgb_ref,
                  hgw_ref, wa_ref, wb_ref, y_ref):
    zb = zb_ref[...]
    gate_b = zb * jax.nn.sigmoid(zb)
    osum = of_ref[...] + ob_ref[...]
    parts = []
    for h in range(HG_HEADS):
        sl = slice(h * HG_DIM, (h + 1) * HG_DIM)
        oh = osum[:, sl]
        ms = jnp.mean(oh * oh, axis=-1, keepdims=True)
        parts.append(oh * lax.rsqrt(ms + NORM_EPS) * hgw_ref[:, sl])
    o_b = (jnp.concatenate(parts, axis=-1) * gate_b).astype(jnp.bfloat16)

    pa = jnp.dot(oa_ref[...], wa_ref[...], preferred_element_type=jnp.float32)
    pb = jnp.dot(o_b, wb_ref[...], preferred_element_type=jnp.float32)
    y = (jax.nn.sigmoid(ga_ref[...] + bga_ref[...]) * pa
         + jax.nn.sigmoid(gb_ref[...] + bgb_ref[...]) * pb)
    y_ref[...] = y.astype(y_ref.dtype)


def _out_proj_kernel(x_ref, y_ref, wo_ref, pw_ref, o_ref):
    u = jnp.dot(y_ref[...], wo_ref[...], preferred_element_type=jnp.float32)
    ms = jnp.mean(u * u, axis=-1, keepdims=True)
    o_ref[...] = x_ref[...] + u * lax.rsqrt(ms + NORM_EPS) * pw_ref[...]


def _const_spec(shape, r=0, c=0):
    return pl.BlockSpec(shape, lambda i: (r, c), pipeline_mode=pl.Buffered(1))


def _merge(oa, o_f, o_b, rest, b_gate, hg_norm_w, w_br, tm=512):
    t, d = oa.shape[0], w_br.shape[1]
    tile = lambda w, c: pl.BlockSpec((tm, w), lambda i: (i, c))
    zb_col = 5
    ga_col = (NA_WIDTH + 5 * HG_WIDTH) // D_MODEL
    bg = b_gate.reshape(1, 2 * d)
    return pl.pallas_call(
        _merge_kernel,
        grid=(t // tm,),
        in_specs=[tile(NA_WIDTH, 0), tile(HG_WIDTH, 0), tile(HG_WIDTH, 0),
                  tile(HG_WIDTH, zb_col), tile(d, ga_col), tile(d, ga_col + 1),
                  _const_spec((1, d)), _const_spec((1, d), 0, 1), _const_spec((1, HG_WIDTH)),
                  _const_spec((NA_WIDTH, d)), _const_spec((HG_WIDTH, d), NA_WIDTH // HG_WIDTH, 0)],
        out_specs=tile(d, 0),
        out_shape=jax.ShapeDtypeStruct((t, d), jnp.bfloat16),
        compiler_params=_params("parallel"),
        name="branch_merge",
    )(oa, o_f, o_b, rest, rest, rest, bg, bg, hg_norm_w.reshape(1, HG_WIDTH), w_br, w_br)


def _out_proj(x, y, w_o, post_w, tm=512):
    t, d = x.shape
    tile = pl.BlockSpec((tm, d), lambda i: (i, 0))
    return pl.pallas_call(
        _out_proj_kernel,
        grid=(t // tm,),
        in_specs=[tile, tile, _const_spec((d, d)), _const_spec((1, d))],
        out_specs=tile,
        out_shape=jax.ShapeDtypeStruct((t, d), jnp.float32),
        compiler_params=_params("parallel"),
        name="out_proj_norm",
    )(x, y, w_o, post_w.reshape(1, d))


def kernel(x, norm_pre_w, w_in, b_gate, na_rel_bias, hg_lb_logits, hg_norm_w, w_branch, w_out, norm_post_w):
    b, t, d = x.shape
    depth = w_in.shape[0]
    bf16 = jnp.bfloat16
    outs = []
    for bi in range(b):
        xb = x[bi]
        for l in range(depth):
            xn = _rmsnorm(xb, norm_pre_w[l])
            qkv = _in_proj(xn, w_in[l], 0, QKV_COLS, bf16, "in_proj_qkv",
                           first_block_scale=LOG2_E * NA_HEAD_DIM ** -0.5)
            rest = _in_proj(xn, w_in[l], QKV_COLS, REST_COLS, jnp.float32, "in_proj_rest")
            oa = _neighbourhood_attention(qkv, rest, _na_bias_table(na_rel_bias[l]))
            o_f, o_b = _hgrn2(rest, hg_lb_logits.astype(jnp.float32), l)
            y = _merge(oa, o_f, o_b, rest, b_gate[l], hg_norm_w[l], w_branch[l].astype(bf16))
            xb = _out_proj(xb, y, w_out[l].astype(bf16), norm_post_w[l])
        outs.append(xb)
    return outs[0][None] if b == 1 else jnp.stack(outs, axis=0)
```

```python
import functools

import jax
import jax.numpy as jnp
import numpy as np
from jax import lax
from jax.experimental import pallas as pl
from jax.experimental.pallas import tpu as pltpu

D_MODEL = 2048
GRID_W = 64
NA_HEAD_DIM = 64
NA_WIDTH = 1024
NA_HEADS = 16
NA_WIN_ROWS = 8
NA_WIN_COLS = 16
HG_DIM = 128
HG_HEADS = 8
HG_WIDTH = 1024
NORM_EPS = 1e-6
QKV_COLS = 3 * NA_WIDTH
REST_COLS = NA_WIDTH + 5 * HG_WIDTH + 2 * D_MODEL
MASK_VALUE = -1e30
LOG2_E = 1.4426950408889634

VMEM_LIMIT_BYTES = 56 * 1024 * 1024


def _params(*sem, flags=None):
    return pltpu.CompilerParams(dimension_semantics=sem, vmem_limit_bytes=VMEM_LIMIT_BYTES, flags=flags)


def _rmsnorm_kernel(x_ref, w_ref, o_ref):
    x = x_ref[...]
    ms = jnp.mean(x * x, axis=-1, keepdims=True)
    o_ref[...] = (x * lax.rsqrt(ms + NORM_EPS) * w_ref[...]).astype(o_ref.dtype)


def _rmsnorm(x, w, tm=1024):
    t, d = x.shape
    return pl.pallas_call(
        _rmsnorm_kernel,
        grid=(t // tm,),
        in_specs=[pl.BlockSpec((tm, d), lambda i: (i, 0)),
                  pl.BlockSpec((1, d), lambda i: (0, 0))],
        out_specs=pl.BlockSpec((tm, d), lambda i: (i, 0)),
        out_shape=jax.ShapeDtypeStruct((t, d), jnp.bfloat16),
        compiler_params=_params("parallel"),
        name="rmsnorm_pre",
    )(x, w.reshape(1, d))


def _in_proj_kernel(a_ref, w_ref, o_ref, wb_ref, *, first_block_scale):
    @pl.when(pl.program_id(1) == 0)
    def _():
        wb_ref[...] = w_ref[...].astype(wb_ref.dtype)

    acc = jnp.dot(a_ref[...], wb_ref[...], preferred_element_type=jnp.float32)
    if first_block_scale is not None:
        acc = acc * jnp.where(pl.program_id(0) == 0, first_block_scale, 1.0)
    o_ref[...] = acc.astype(o_ref.dtype)


def _in_proj(a, w, col0, ncols, out_dtype, name, first_block_scale=None, tm=1024, tn=1024):
    m, k = a.shape
    assert col0 % tn == 0 and ncols % tn == 0 and m % tm == 0
    j0 = col0 // tn
    return pl.pallas_call(
        functools.partial(_in_proj_kernel, first_block_scale=first_block_scale),
        grid=(ncols // tn, m // tm),
        in_specs=[pl.BlockSpec((tm, k), lambda j, i: (i, 0)),
                  pl.BlockSpec((k, tn), lambda j, i: (0, j0 + j))],
        out_specs=pl.BlockSpec((tm, tn), lambda j, i: (i, j)),
        out_shape=jax.ShapeDtypeStruct((m, ncols), out_dtype),
        scratch_shapes=[pltpu.VMEM((k, tn), a.dtype)],
        compiler_params=_params("arbitrary", "arbitrary"),
        name=name,
    )(a, w)


N_SILU_BLOCKS = 2


def _gate_proj_kernel(a_ref, w_ref, b_ref, o_ref, wb_ref):
    @pl.when(pl.program_id(1) == 0)
    def _():
        wb_ref[...] = w_ref[...].astype(wb_ref.dtype)

    is_silu = pl.program_id(0) < N_SILU_BLOCKS
    bias = jnp.where(is_silu, 0.0, b_ref[...])
    rows = 256
    for r0 in range(0, a_ref.shape[0], rows):
        z = jnp.dot(a_ref[r0:r0 + rows, :], wb_ref[...], preferred_element_type=jnp.float32) + bias
        sg = jax.nn.sigmoid(z)
        o_ref[r0:r0 + rows, :] = jnp.where(is_silu, z * sg, sg).astype(o_ref.dtype)


def _gate_proj(a, w, b_gate, tm=1024, tn=1024):
    m, k = a.shape
    n_gate = b_gate.shape[0] // tn
    nblk = N_SILU_BLOCKS + n_gate
    na_z_blk = QKV_COLS // tn
    hg_z_blk = (QKV_COLS + NA_WIDTH + 4 * HG_WIDTH) // tn
    w_blk = lambda j: jnp.where(j == 0, na_z_blk, hg_z_blk - 1 + j)
    return pl.pallas_call(
        _gate_proj_kernel,
        grid=(nblk, m // tm),
        in_specs=[pl.BlockSpec((tm, k), lambda j, i: (i, 0)),
                  pl.BlockSpec((k, tn), lambda j, i: (0, w_blk(j))),
                  pl.BlockSpec((1, tn), lambda j, i: (0, jnp.maximum(j - N_SILU_BLOCKS, 0)))],
        out_specs=pl.BlockSpec((tm, tn), lambda j, i: (i, j)),
        out_shape=jax.ShapeDtypeStruct((m, nblk * tn), jnp.bfloat16),
        scratch_shapes=[pltpu.VMEM((k, tn), a.dtype)],
        compiler_params=_params("arbitrary", "arbitrary"),
        name="in_proj_gates",
    )(a, w, b_gate.reshape(1, -1))


def _na_bias_table(rpb):
    qc = np.arange(GRID_W)[:, None]
    kc = np.arange(GRID_W)[None, :]
    cs = np.clip(qc - NA_WIN_COLS // 2, 0, GRID_W - NA_WIN_COLS)
    col_mask = (kc >= cs) & (kc < cs + NA_WIN_COLS)
    col_idx = np.clip(kc - qc + NA_WIN_COLS - 1, 0, 2 * NA_WIN_COLS - 2)
    pick = (col_idx[:, :, None] == np.arange(2 * NA_WIN_COLS - 1)) & col_mask[:, :, None]
    tz = jnp.einsum("hrc,qkc->hqrk", rpb.astype(jnp.float32) * LOG2_E, jnp.asarray(pick, jnp.float32),
                    precision=lax.Precision.HIGHEST)
    tz = tz + jnp.asarray(np.where(col_mask, 0.0, MASK_VALUE)[None, :, None, :], jnp.float32)
    band = NA_WIN_ROWS * GRID_W
    variants = [tz[:, :, NA_WIN_ROWS - 1 - var: 2 * NA_WIN_ROWS - 1 - var, :]
                .reshape(NA_HEADS, GRID_W, band) for var in range(NA_WIN_ROWS)]
    return jnp.stack(variants, axis=0)


def _na_kernel(q_ref, k0_ref, v0_ref, kn_ref, vn_ref, gate_ref, bias_ref, o_ref,
               k_ring, v_ring, s_ref, m_ref, *, rows):
    lanes = 2 * NA_HEAD_DIM
    band = NA_WIN_ROWS * GRID_W
    half = NA_WIN_ROWS // 2
    first = lax.broadcasted_iota(jnp.int32, (GRID_W, lanes), 1) < NA_HEAD_DIM
    pairs = NA_HEADS // 2
    ones = jnp.ones((band, lanes), jnp.bfloat16)

    r = pl.program_id(0)

    @pl.when(r == 0)
    def _():
        k_ring[...] = k0_ref[...]
        v_ring[...] = v0_ref[...]

    @pl.when((r > half) & (r <= rows - half))
    def _():
        slot = (r + half - 1) & (NA_WIN_ROWS - 1)
        k_ring[slot] = kn_ref[0]
        v_ring[slot] = vn_ref[0]

    start = jnp.clip(r - half, 0, rows - NA_WIN_ROWS)
    slots = [(start + j) & (NA_WIN_ROWS - 1) for j in range(NA_WIN_ROWS)]

    for p in range(pairs):
        sl = slice(p * lanes, (p + 1) * lanes)
        qp = q_ref[:, sl]
        zero = jnp.zeros_like(qp)
        q_bd = jnp.concatenate([jnp.where(first, qp, zero), jnp.where(first, zero, qp)], axis=0)
        k_band = jnp.concatenate([k_ring[j, :, sl] for j in slots], axis=0)
        s = lax.dot_general(q_bd, k_band, (((1,), (1,)), ((), ())),
                            preferred_element_type=jnp.float32) + bias_ref[0, p]
        s_ref[p] = s
        m_ref[p] = jnp.broadcast_to(jnp.max(s, axis=-1, keepdims=True), (2 * GRID_W, lanes))
    for p in range(pairs):
        sl = slice(p * lanes, (p + 1) * lanes)
        m = m_ref[p]
        e = jnp.exp2(s_ref[p] - jnp.concatenate([m] * (band // lanes), axis=-1))
        v_band = jnp.concatenate([v_ring[j, :, sl] for j in slots], axis=0)
        v_ext = jnp.concatenate([v_band, ones], axis=1)
        ol = jnp.dot(e.astype(jnp.bfloat16), v_ext, preferred_element_type=jnp.float32)
        o = ol[:, :lanes] / ol[:, lanes:]
        o = jnp.where(first, o[:GRID_W], o[GRID_W:])
        o_ref[:, sl] = (o * gate_ref[:, sl].astype(jnp.float32)).astype(o_ref.dtype)


def _neighbourhood_attention(qkv, gates, bias_tab):
    t = qkv.shape[0]
    rows = t // GRID_W
    band = NA_WIN_ROWS * GRID_W

    def row_start(r):
        return jnp.clip(r - NA_WIN_ROWS // 2, 0, rows - NA_WIN_ROWS)

    row_blk = (GRID_W, NA_WIDTH)
    ring = (NA_WIN_ROWS, GRID_W, NA_WIDTH)
    half = NA_WIN_ROWS // 2
    first_rows = [pl.BlockSpec(ring, lambda r, c=c: (0, 0, c), pipeline_mode=pl.Buffered(1)) for c in (1, 2)]
    new_row = [pl.BlockSpec((1,) + row_blk,
                            lambda r, c=c: (jnp.clip(r + half - 1, NA_WIN_ROWS - 1, rows - 1), 0, c))
               for c in (1, 2)]
    qkv3 = qkv.reshape(rows, GRID_W, QKV_COLS)
    return pl.pallas_call(
        functools.partial(_na_kernel, rows=rows),
        grid=(rows,),
        in_specs=[pl.BlockSpec(row_blk, lambda r: (r, 0))] + first_rows + new_row + [
            pl.BlockSpec(row_blk, lambda r: (r, 0)),
            pl.BlockSpec((1, NA_HEADS // 2, 2 * GRID_W, band), lambda r: (r - row_start(r), 0, 0, 0)),
        ],
        out_specs=pl.BlockSpec(row_blk, lambda r: (r, 0)),
        out_shape=jax.ShapeDtypeStruct((t, NA_WIDTH), jnp.bfloat16),
        scratch_shapes=[pltpu.VMEM(ring, qkv.dtype), pltpu.VMEM(ring, qkv.dtype),
                        pltpu.VMEM((NA_HEADS // 2, 2 * GRID_W, band), jnp.float32),
                        pltpu.VMEM((NA_HEADS // 2, 2 * GRID_W, 2 * NA_HEAD_DIM), jnp.float32)],
        compiler_params=_params("arbitrary"),
        name="neighbourhood_attention",
    )(qkv, qkv3, qkv3, qkv3, qkv3, gates,
      bias_tab.reshape(NA_WIN_ROWS, NA_HEADS // 2, 2 * GRID_W, band))


HG_CHUNK = 64
HG_BLOCK = 256


SUBLANES = 8


def _hgrn_level_masks():
    c = HG_CHUNK
    t = np.arange(c)[:, None]
    s = np.arange(c)[None, :]
    out = []
    for reverse in (False, True):
        tt, ss = (c - 1 - t, c - 1 - s) if reverse else (t, s)
        x = tt ^ ss
        levels = [tt == ss]
        m = 1
        while m < c:
            levels.append((tt > ss) & (x >= m) & (x < 2 * m))
            m *= 2
        if len(levels) % 2:
            levels.append(np.zeros((c, c), bool))
        out.append(np.stack([np.concatenate(levels[i:i + 2], axis=1) for i in range(0, len(levels), 2)]))
    return np.stack(out).astype(np.float32)


def _hgrn_chunk(q, fl, v, lb, mask_ref, reverse):
    c, sub = HG_CHUNK, SUBLANES
    nv = c // sub
    bf16 = jnp.bfloat16
    en = jnp.exp2(jnp.abs(fl) * (-LOG2_E))
    big = 1.0 / (1.0 + en)
    small = en * big
    nonneg = fl >= 0.0
    f = lb + (1.0 - lb) * jnp.where(nonneg, big, small)
    g = jnp.log2(f)
    k = (1.0 - lb) * jnp.where(nonneg, small, big)

    srow = lax.broadcasted_iota(jnp.int32, (sub, HG_DIM), 0)
    cp = (sub - 1 - srow) if reverse else srow

    def prev_shift(x, j):
        return pltpu.roll(x, (sub - j) if reverse else j, 0)

    def row_bcast(x, p):
        i = (sub - 1 - p) if reverse else p
        return jnp.broadcast_to(x[i:i + 1, :], (sub, HG_DIM))

    def rows(lst):
        return jnp.concatenate(lst[::-1] if reverse else lst, axis=0)

    blocks = range(nv - 1, -1, -1) if reverse else range(nv)
    gs = [g[b * sub:(b + 1) * sub] for b in blocks]
    loc = []
    for x in gs:
        j = 1
        while j < sub:
            x = x + jnp.where(cp >= j, prev_shift(x, j), 0.0)
            j *= 2
        loc.append(x)
    carry = [None, row_bcast(loc[0], sub - 1)]
    a = [loc[0]]
    for i in range(1, nv):
        a.append(loc[i] + carry[i])
        carry.append(carry[i] + row_bcast(loc[i], sub - 1))
    total = carry[nv]

    def level_exponents(m):
        if m >= sub:
            w = m // sub
            out = []
            for i in range(nv):
                mid = carry[(i // (2 * w)) * (2 * w) + w]
                out.append(a[i] - mid if (i // w) % 2 else mid - a[i])
            return out
        if m == 1:
            return [jnp.where((cp & 1) == 1, x, 0.0) for x in gs]
        out = []
        for x in a:
            r = row_bcast(x, m - 1)
            for blk in range(1, sub // (2 * m)):
                r = jnp.where(cp < blk * 2 * m, r, row_bcast(x, blk * 2 * m + m - 1))
            out.append(-jnp.abs(x - r))
        return out

    d = 1 if reverse else 0
    qb, kb = q.astype(bf16), k.astype(bf16)
    groups = [(qb, kb)]
    m = 1
    while m < c:
        e = jnp.exp2(rows(level_exponents(m))).astype(bf16)
        groups.append((qb * e, kb * e))
        m *= 2

    zeros = jnp.zeros((c, HG_DIM), bf16)
    scores = None
    for slab in range(0, len(groups), 2):
        qa, ka = groups[slab]
        if slab + 1 < len(groups):
            qn, kn = groups[slab + 1]
            lhs = jnp.concatenate([qa, qn], axis=1)
            rhs = jnp.concatenate([jnp.concatenate([ka, zeros], axis=1),
                                   jnp.concatenate([zeros, kn], axis=1)], axis=0)
        else:
            lhs, rhs = qa, jnp.concatenate([ka, zeros], axis=0)
        part = mask_ref[d, slab // 2] * lax.dot_general(lhs, rhs, (((1,), (1,)), ((), ())),
                                                         preferred_element_type=jnp.float32)
        scores = part if scores is None else scores + part

    q_in = (q * jnp.exp2(rows(a))).astype(bf16)
    k_out = (k * jnp.exp2(rows([total - x for x in a]))).astype(bf16)
    vb = v.astype(bf16)
    return (scores.astype(bf16), q_in, k_out, jnp.exp2(total[0:1, :]),
            jnp.concatenate([vb, vb], axis=0), v.T.astype(bf16))


def _hgrn_chunk_state(intra, st):
    scores, q_in, k_out, decay, vv, vtb = intra
    o = jnp.dot(scores, vv, preferred_element_type=jnp.float32)
    o = o + lax.dot_general(q_in, st.astype(jnp.bfloat16), (((1,), (1,)), ((), ())),
                            preferred_element_type=jnp.float32)
    st_new = st * decay + jnp.dot(vtb, k_out, preferred_element_type=jnp.float32)
    return o, st_new


def _hgrn_kernel(lbl_ref, mask_ref, qf_ref, ff_ref, vf_ref, qb_ref, fb_ref, vb_ref,
                 of_ref, ob_ref, st_ref, *, layer):
    @pl.when(pl.program_id(0) == 0)
    def _():
        st_ref[...] = jnp.zeros_like(st_ref)

    lg = lbl_ref[...]
    mx = jnp.max(lg, axis=0, keepdims=True)
    ex = jnp.exp(lg - mx)
    lb = jnp.sum(ex[: layer + 1], axis=0) / jnp.sum(ex, axis=0)

    n = HG_BLOCK // HG_CHUNK
    refs = ((qf_ref, ff_ref, vf_ref, of_ref), (qb_ref, fb_ref, vb_ref, ob_ref))
    state = {}
    pending = None

    def finish(job):
        d, h, ci, rows, cols, intra = job
        st = st_ref[d, h] if ci == 0 else state[d, h]
        o, st = _hgrn_chunk_state(intra, st)
        refs[d][3][rows, cols] = o.astype(refs[d][3].dtype)
        if ci == n - 1:
            st_ref[d, h] = st
        else:
            state[d, h] = st

    for ci in range(n):
        for h in range(HG_HEADS):
            cols = slice(h * HG_DIM, (h + 1) * HG_DIM)
            for d in (0, 1):
                blk = ci if d == 0 else n - 1 - ci
                rows = slice(blk * HG_CHUNK, (blk + 1) * HG_CHUNK)
                q_ref, f_ref, v_ref, _ = refs[d]
                intra = _hgrn_chunk(q_ref[rows, cols], f_ref[rows, cols], v_ref[rows, cols],
                                    lb[d:d + 1, cols], mask_ref, d == 1)
                if pending is not None:
                    finish(pending)
                pending = (d, h, ci, rows, cols, intra)
    finish(pending)


def _hgrn2(hg, lb_logits, layer):
    rest = hg
    t = rest.shape[0]
    nb = t // HG_BLOCK
    layers = lb_logits.shape[0]
    masks = jnp.asarray(_hgrn_level_masks())
    q_c, ff_c, fb_c, i_c = 0, 1, 2, 3
    blk = (HG_BLOCK, HG_WIDTH)
    fwd = lambda c: pl.BlockSpec(blk, lambda b: (b, c))
    bwd = lambda c: pl.BlockSpec(blk, lambda b: (nb - 1 - b, c))
    return pl.pallas_call(
        functools.partial(_hgrn_kernel, layer=layer),
        grid=(nb,),
        in_specs=[pl.BlockSpec((layers, 2, HG_WIDTH), lambda b: (0, 0, 0)),
                  pl.BlockSpec(masks.shape, lambda b: (0, 0, 0, 0)),
                  fwd(q_c), fwd(ff_c), fwd(i_c), bwd(q_c), bwd(fb_c), bwd(i_c)],
        out_specs=[pl.BlockSpec(blk, lambda b: (b, 0)),
                   pl.BlockSpec(blk, lambda b: (nb - 1 - b, 0))],
        out_shape=[jax.ShapeDtypeStruct((t, HG_WIDTH), jnp.bfloat16)] * 2,
        scratch_shapes=[pltpu.VMEM((2, HG_HEADS, HG_DIM, HG_DIM), jnp.float32)],
        compiler_params=_params("arbitrary"),
        name="hgrn2_bidirectional",
    )(lb_logits, masks, rest, rest, rest, rest, rest, rest)


def _merge_kernel(oa_ref, of_ref, ob_ref, gate_b_ref, ga_ref, gb_ref, hgw_ref, wa_ref, wb_ref, y_ref):
    f32 = jnp.float32
    osum = of_ref[...].astype(f32) + ob_ref[...].astype(f32)
    parts = []
    for h in range(HG_HEADS):
        sl = slice(h * HG_DIM, (h + 1) * HG_DIM)
        oh = osum[:, sl]
        ms = jnp.mean(oh * oh, axis=-1, keepdims=True)
        parts.append(oh * lax.rsqrt(ms + NORM_EPS) * hgw_ref[:, sl])
    o_b = (jnp.concatenate(parts, axis=-1) * gate_b_ref[...].astype(f32)).astype(jnp.bfloat16)

    pa = jnp.dot(oa_ref[...], wa_ref[...], preferred_element_type=f32)
    pb = jnp.dot(o_b, wb_ref[...], preferred_element_type=f32)
    y = ga_ref[...].astype(f32) * pa + gb_ref[...].astype(f32) * pb
    y_ref[...] = y.astype(y_ref.dtype)


def _out_proj_kernel(x_ref, y_ref, wo_ref, pw_ref, o_ref):
    u = jnp.dot(y_ref[...], wo_ref[...], preferred_element_type=jnp.float32)
    ms = jnp.mean(u * u, axis=-1, keepdims=True)
    o_ref[...] = x_ref[...] + u * lax.rsqrt(ms + NORM_EPS) * pw_ref[...]


def _const_spec(shape, r=0, c=0):
    return pl.BlockSpec(shape, lambda i: (r, c), pipeline_mode=pl.Buffered(1))


def _merge(oa, o_f, o_b, gates, hg_norm_w, w_br, tm=512):
    t, d = oa.shape[0], w_br.shape[1]
    tile = lambda w, c: pl.BlockSpec((tm, w), lambda i: (i, c))
    ga_col = N_SILU_BLOCKS * HG_WIDTH // d
    return pl.pallas_call(
        _merge_kernel,
        grid=(t // tm,),
        in_specs=[tile(NA_WIDTH, 0), tile(HG_WIDTH, 0), tile(HG_WIDTH, 0),
                  tile(HG_WIDTH, 1), tile(d, ga_col), tile(d, ga_col + 1),
                  _const_spec((1, HG_WIDTH)),
                  _const_spec((NA_WIDTH, d)), _const_spec((HG_WIDTH, d), NA_WIDTH // HG_WIDTH, 0)],
        out_specs=tile(d, 0),
        out_shape=jax.ShapeDtypeStruct((t, d), jnp.bfloat16),
        compiler_params=_params("parallel"),
        name="branch_merge",
    )(oa, o_f, o_b, gates, gates, gates, hg_norm_w.reshape(1, HG_WIDTH), w_br, w_br)


def _out_proj(x, y, w_o, post_w, tm=512):
    t, d = x.shape
    tile = pl.BlockSpec((tm, d), lambda i: (i, 0))
    return pl.pallas_call(
        _out_proj_kernel,
        grid=(t // tm,),
        in_specs=[tile, tile, _const_spec((d, d)), _const_spec((1, d))],
        out_specs=tile,
        out_shape=jax.ShapeDtypeStruct((t, d), jnp.float32),
        compiler_params=_params("parallel"),
        name="out_proj_norm",
    )(x, y, w_o, post_w.reshape(1, d))


def kernel(x, norm_pre_w, w_in, b_gate, na_rel_bias, hg_lb_logits, hg_norm_w, w_branch, w_out, norm_post_w):
    b, t, d = x.shape
    depth = w_in.shape[0]
    bf16 = jnp.bfloat16
    outs = []
    for bi in range(b):
        xb = x[bi]
        for l in range(depth):
            xn = _rmsnorm(xb, norm_pre_w[l])
            qkv = _in_proj(xn, w_in[l], 0, QKV_COLS, bf16, "in_proj_qkv",
                           first_block_scale=LOG2_E * NA_HEAD_DIM ** -0.5)
            hg = _in_proj(xn, w_in[l], QKV_COLS + NA_WIDTH, 4 * HG_WIDTH, jnp.float32, "in_proj_hg")
            gates = _gate_proj(xn, w_in[l], b_gate[l])
            oa = _neighbourhood_attention(qkv, gates, _na_bias_table(na_rel_bias[l]))
            o_f, o_b = _hgrn2(hg, hg_lb_logits.astype(jnp.float32), l)
            y = _merge(oa, o_f, o_b, gates, hg_norm_w[l], w_branch[l].astype(bf16))
            xb = _out_proj(xb, y, w_out[l].astype(bf16), norm_post_w[l])
        outs.append(xb)
    return outs[0][None] if b == 1 else jnp.stack(outs, axis=0)
```

```python
import functools

import jax
import jax.numpy as jnp
import numpy as np
from jax import lax
from jax.experimental import pallas as pl
from jax.experimental.pallas import tpu as pltpu

D_MODEL = 2048
GRID_W = 64
NA_HEAD_DIM = 64
NA_WIDTH = 1024
NA_HEADS = 16
NA_WIN_ROWS = 8
NA_WIN_COLS = 16
HG_DIM = 128
HG_HEADS = 8
HG_WIDTH = 1024
NORM_EPS = 1e-6
QKV_COLS = 3 * NA_WIDTH
MASK_VALUE = -1e30
LOG2_E = 1.4426950408889634

VMEM_LIMIT_BYTES = 56 * 1024 * 1024


def _params(*sem, flags=None):
    return pltpu.CompilerParams(dimension_semantics=sem, vmem_limit_bytes=VMEM_LIMIT_BYTES, flags=flags)


def _rmsnorm_kernel(x_ref, w_ref, o_ref):
    x = x_ref[...]
    ms = jnp.mean(x * x, axis=-1, keepdims=True)
    o_ref[...] = (x * lax.rsqrt(ms + NORM_EPS) * w_ref[...]).astype(o_ref.dtype)


def _rmsnorm(x, w, tm=1024):
    t, d = x.shape
    return pl.pallas_call(
        _rmsnorm_kernel,
        grid=(t // tm,),
        in_specs=[pl.BlockSpec((tm, d), lambda i: (i, 0)),
                  pl.BlockSpec((1, d), lambda i: (0, 0))],
        out_specs=pl.BlockSpec((tm, d), lambda i: (i, 0)),
        out_shape=jax.ShapeDtypeStruct((t, d), jnp.bfloat16),
        compiler_params=_params("parallel"),
        name="rmsnorm_pre",
    )(x, w.reshape(1, d))


def _in_proj_kernel(a_ref, w_ref, o_ref, wb_ref, *, first_block_scale):
    @pl.when(pl.program_id(1) == 0)
    def _():
        wb_ref[...] = w_ref[...].astype(wb_ref.dtype)

    acc = jnp.dot(a_ref[...], wb_ref[...], preferred_element_type=jnp.float32)
    if first_block_scale is not None:
        acc = acc * jnp.where(pl.program_id(0) == 0, first_block_scale, 1.0)
    o_ref[...] = acc.astype(o_ref.dtype)


def _in_proj(a, w, col0, ncols, out_dtype, name, first_block_scale=None, rotate=0, tm=1024, tn=1024):
    m, k = a.shape
    assert col0 % tn == 0 and ncols % tn == 0 and m % tm == 0
    j0 = col0 // tn
    nblk = ncols // tn
    out_blk = lambda j: jnp.where(j < rotate, j + nblk - rotate, j - rotate)
    return pl.pallas_call(
        functools.partial(_in_proj_kernel, first_block_scale=first_block_scale),
        grid=(nblk, m // tm),
        in_specs=[pl.BlockSpec((tm, k), lambda j, i: (i, 0)),
                  pl.BlockSpec((k, tn), lambda j, i: (0, j0 + j))],
        out_specs=pl.BlockSpec((tm, tn), lambda j, i: (i, out_blk(j))),
        out_shape=jax.ShapeDtypeStruct((m, ncols), out_dtype),
        scratch_shapes=[pltpu.VMEM((k, tn), a.dtype)],
        compiler_params=_params("arbitrary", "arbitrary"),
        name=name,
    )(a, w)


def _na_bias_table(rpb):
    qc = np.arange(GRID_W)[:, None]
    kc = np.arange(GRID_W)[None, :]
    cs = np.clip(qc - NA_WIN_COLS // 2, 0, GRID_W - NA_WIN_COLS)
    col_mask = (kc >= cs) & (kc < cs + NA_WIN_COLS)
    col_idx = np.clip(kc - qc + NA_WIN_COLS - 1, 0, 2 * NA_WIN_COLS - 2)
    pick = (col_idx[:, :, None] == np.arange(2 * NA_WIN_COLS - 1)) & col_mask[:, :, None]
    tz = jnp.einsum("hrc,qkc->hqrk", rpb.astype(jnp.float32) * LOG2_E, jnp.asarray(pick, jnp.float32),
                    precision=lax.Precision.HIGHEST)
    tz = tz + jnp.asarray(np.where(col_mask, 0.0, MASK_VALUE)[None, :, None, :], jnp.float32)
    band = NA_WIN_ROWS * GRID_W
    variants = [tz[:, :, NA_WIN_ROWS - 1 - var: 2 * NA_WIN_ROWS - 1 - var, :]
                .reshape(NA_HEADS, GRID_W, band) for var in range(NA_WIN_ROWS)]
    return jnp.stack(variants, axis=0)


def _na_kernel(q_ref, k0_ref, v0_ref, kn_ref, vn_ref, z_ref, bias_ref, o_ref,
               k_ring, v_ring, s_ref, m_ref, *, rows):
    lanes = 2 * NA_HEAD_DIM
    band = NA_WIN_ROWS * GRID_W
    half = NA_WIN_ROWS // 2
    first = lax.broadcasted_iota(jnp.int32, (GRID_W, lanes), 1) < NA_HEAD_DIM
    pairs = NA_HEADS // 2
    ones = jnp.ones((band, lanes), jnp.bfloat16)

    r = pl.program_id(0)

    @pl.when(r == 0)
    def _():
        k_ring[...] = k0_ref[...]
        v_ring[...] = v0_ref[...]

    @pl.when((r > half) & (r <= rows - half))
    def _():
        slot = (r + half - 1) & (NA_WIN_ROWS - 1)
        k_ring[slot] = kn_ref[0]
        v_ring[slot] = vn_ref[0]

    start = jnp.clip(r - half, 0, rows - NA_WIN_ROWS)
    slots = [(start + j) & (NA_WIN_ROWS - 1) for j in range(NA_WIN_ROWS)]

    for p in range(pairs):
        sl = slice(p * lanes, (p + 1) * lanes)
        qp = q_ref[:, sl]
        zero = jnp.zeros_like(qp)
        q_bd = jnp.concatenate([jnp.where(first, qp, zero), jnp.where(first, zero, qp)], axis=0)
        k_band = jnp.concatenate([k_ring[j, :, sl] for j in slots], axis=0)
        s = lax.dot_general(q_bd, k_band, (((1,), (1,)), ((), ())),
                            preferred_element_type=jnp.float32) + bias_ref[0, p]
        s_ref[p] = s
        m_ref[p] = jnp.broadcast_to(jnp.max(s, axis=-1, keepdims=True), (2 * GRID_W, lanes))
    for p in range(pairs):
        sl = slice(p * lanes, (p + 1) * lanes)
        m = m_ref[p]
        e = jnp.exp2(s_ref[p] - jnp.concatenate([m] * (band // lanes), axis=-1))
        v_band = jnp.concatenate([v_ring[j, :, sl] for j in slots], axis=0)
        v_ext = jnp.concatenate([v_band, ones], axis=1)
        ol = jnp.dot(e.astype(jnp.bfloat16), v_ext, preferred_element_type=jnp.float32)
        o = ol[:, :lanes] / ol[:, lanes:]
        o = jnp.where(first, o[:GRID_W], o[GRID_W:])
        z = z_ref[:, sl]
        o_ref[:, sl] = (o * (z * jax.nn.sigmoid(z))).astype(o_ref.dtype)


def _neighbourhood_attention(qkv, gates, bias_tab):
    t = qkv.shape[0]
    rows = t // GRID_W
    band = NA_WIN_ROWS * GRID_W

    def row_start(r):
        return jnp.clip(r - NA_WIN_ROWS // 2, 0, rows - NA_WIN_ROWS)

    row_blk = (GRID_W, NA_WIDTH)
    ring = (NA_WIN_ROWS, GRID_W, NA_WIDTH)
    half = NA_WIN_ROWS // 2
    first_rows = [pl.BlockSpec(ring, lambda r, c=c: (0, 0, c), pipeline_mode=pl.Buffered(1)) for c in (1, 2)]
    new_row = [pl.BlockSpec((1,) + row_blk,
                            lambda r, c=c: (jnp.clip(r + half - 1, NA_WIN_ROWS - 1, rows - 1), 0, c))
               for c in (1, 2)]
    qkv3 = qkv.reshape(rows, GRID_W, QKV_COLS)
    return pl.pallas_call(
        functools.partial(_na_kernel, rows=rows),
        grid=(rows,),
        in_specs=[pl.BlockSpec(row_blk, lambda r: (r, 0))] + first_rows + new_row + [
            pl.BlockSpec(row_blk, lambda r: (r, 0)),
            pl.BlockSpec((1, NA_HEADS // 2, 2 * GRID_W, band), lambda r: (r - row_start(r), 0, 0, 0)),
        ],
        out_specs=pl.BlockSpec(row_blk, lambda r: (r, 0)),
        out_shape=jax.ShapeDtypeStruct((t, NA_WIDTH), jnp.bfloat16),
        scratch_shapes=[pltpu.VMEM(ring, qkv.dtype), pltpu.VMEM(ring, qkv.dtype),
                        pltpu.VMEM((NA_HEADS // 2, 2 * GRID_W, band), jnp.float32),
                        pltpu.VMEM((NA_HEADS // 2, 2 * GRID_W, 2 * NA_HEAD_DIM), jnp.float32)],
        compiler_params=_params("arbitrary"),
        name="neighbourhood_attention",
    )(qkv, qkv3, qkv3, qkv3, qkv3, gates,
      bias_tab.reshape(NA_WIN_ROWS, NA_HEADS // 2, 2 * GRID_W, band))


HG_CHUNK = 64
HG_BLOCK = 256


SUBLANES = 8


def _hgrn_level_masks():
    c = HG_CHUNK
    t = np.arange(c)[:, None]
    s = np.arange(c)[None, :]
    out = []
    for reverse in (False, True):
        tt, ss = (c - 1 - t, c - 1 - s) if reverse else (t, s)
        x = tt ^ ss
        levels = [tt == ss]
        m = 1
        while m < c:
            levels.append((tt > ss) & (x >= m) & (x < 2 * m))
            m *= 2
        if len(levels) % 2:
            levels.append(np.zeros((c, c), bool))
        out.append(np.stack([np.concatenate(levels[i:i + 2], axis=1) for i in range(0, len(levels), 2)]))
    return np.stack(out).astype(np.float32)


def _hgrn_chunk(q, fl, v, lb, mask_ref, reverse):
    c, sub = HG_CHUNK, SUBLANES
    nv = c // sub
    bf16 = jnp.bfloat16
    en = jnp.exp2(jnp.abs(fl) * (-LOG2_E))
    big = 1.0 / (1.0 + en)
    small = en * big
    nonneg = fl >= 0.0
    f = lb + (1.0 - lb) * jnp.where(nonneg, big, small)
    g = jnp.log2(f)
    k = (1.0 - lb) * jnp.where(nonneg, small, big)

    srow = lax.broadcasted_iota(jnp.int32, (sub, HG_DIM), 0)
    cp = (sub - 1 - srow) if reverse else srow

    def prev_shift(x, j):
        return pltpu.roll(x, (sub - j) if reverse else j, 0)

    def row_bcast(x, p):
        i = (sub - 1 - p) if reverse else p
        return jnp.broadcast_to(x[i:i + 1, :], (sub, HG_DIM))

    def rows(lst):
        return jnp.concatenate(lst[::-1] if reverse else lst, axis=0)

    blocks = range(nv - 1, -1, -1) if reverse else range(nv)
    gs = [g[b * sub:(b + 1) * sub] for b in blocks]
    loc = []
    for x in gs:
        j = 1
        while j < sub:
            x = x + jnp.where(cp >= j, prev_shift(x, j), 0.0)
            j *= 2
        loc.append(x)
    carry = [None, row_bcast(loc[0], sub - 1)]
    a = [loc[0]]
    for i in range(1, nv):
        a.append(loc[i] + carry[i])
        carry.append(carry[i] + row_bcast(loc[i], sub - 1))
    total = carry[nv]

    def level_exponents(m):
        if m >= sub:
            w = m // sub
            out = []
            for i in range(nv):
                mid = carry[(i // (2 * w)) * (2 * w) + w]
                out.append(a[i] - mid if (i // w) % 2 else mid - a[i])
            return out
        if m == 1:
            return [jnp.where((cp & 1) == 1, x, 0.0) for x in gs]
        out = []
        for x in a:
            r = row_bcast(x, m - 1)
            for blk in range(1, sub // (2 * m)):
                r = jnp.where(cp < blk * 2 * m, r, row_bcast(x, blk * 2 * m + m - 1))
            out.append(-jnp.abs(x - r))
        return out

    d = 1 if reverse else 0
    qb, kb = q.astype(bf16), k.astype(bf16)
    groups = [(qb, kb)]
    m = 1
    while m < c:
        e = jnp.exp2(rows(level_exponents(m))).astype(bf16)
        groups.append((qb * e, kb * e))
        m *= 2

    zeros = jnp.zeros((c, HG_DIM), bf16)
    scores = None
    for slab in range(0, len(groups), 2):
        qa, ka = groups[slab]
        if slab + 1 < len(groups):
            qn, kn = groups[slab + 1]
            lhs = jnp.concatenate([qa, qn], axis=1)
            rhs = jnp.concatenate([jnp.concatenate([ka, zeros], axis=1),
                                   jnp.concatenate([zeros, kn], axis=1)], axis=0)
        else:
            lhs, rhs = qa, jnp.concatenate([ka, zeros], axis=0)
        part = mask_ref[d, slab // 2] * lax.dot_general(lhs, rhs, (((1,), (1,)), ((), ())),
                                                         preferred_element_type=jnp.float32)
        scores = part if scores is None else scores + part

    q_in = (q * jnp.exp2(rows(a))).astype(bf16)
    k_out = (k * jnp.exp2(rows([total - x for x in a]))).astype(bf16)
    vb = v.astype(bf16)
    return (scores.astype(bf16), q_in, k_out, jnp.exp2(total[0:1, :]),
            jnp.concatenate([vb, vb], axis=0), v.T.astype(bf16))


def _hgrn_chunk_state(intra, st):
    scores, q_in, k_out, decay, vv, vtb = intra
    o = jnp.dot(scores, vv, preferred_element_type=jnp.float32)
    o = o + lax.dot_general(q_in, st.astype(jnp.bfloat16), (((1,), (1,)), ((), ())),
                            preferred_element_type=jnp.float32)
    st_new = st * decay + jnp.dot(vtb, k_out, preferred_element_type=jnp.float32)
    return o, st_new


def _hgrn_kernel(lbl_ref, mask_ref, qf_ref, ff_ref, vf_ref, qb_ref, fb_ref, vb_ref,
                 of_ref, ob_ref, st_ref, *, layer):
    @pl.when(pl.program_id(0) == 0)
    def _():
        st_ref[...] = jnp.zeros_like(st_ref)

    lg = lbl_ref[...]
    mx = jnp.max(lg, axis=0, keepdims=True)
    ex = jnp.exp(lg - mx)
    lb = jnp.sum(ex[: layer + 1], axis=0) / jnp.sum(ex, axis=0)

    n = HG_BLOCK // HG_CHUNK
    refs = ((qf_ref, ff_ref, vf_ref, of_ref), (qb_ref, fb_ref, vb_ref, ob_ref))
    state = {}
    pending = None

    def finish(job):
        d, h, ci, rows, cols, intra = job
        st = st_ref[d, h] if ci == 0 else state[d, h]
        o, st = _hgrn_chunk_state(intra, st)
        refs[d][3][rows, cols] = o.astype(refs[d][3].dtype)
        if ci == n - 1:
            st_ref[d, h] = st
        else:
            state[d, h] = st

    for ci in range(n):
        for h in range(HG_HEADS):
            cols = slice(h * HG_DIM, (h + 1) * HG_DIM)
            for d in (0, 1):
                blk = ci if d == 0 else n - 1 - ci
                rows = slice(blk * HG_CHUNK, (blk + 1) * HG_CHUNK)
                q_ref, f_ref, v_ref, _ = refs[d]
                intra = _hgrn_chunk(q_ref[rows, cols], f_ref[rows, cols], v_ref[rows, cols],
                                    lb[d:d + 1, cols], mask_ref, d == 1)
                if pending is not None:
                    finish(pending)
                pending = (d, h, ci, rows, cols, intra)
    finish(pending)


def _hgrn2(rest, lb_logits, layer):
    t = rest.shape[0]
    nb = t // HG_BLOCK
    layers = lb_logits.shape[0]
    masks = jnp.asarray(_hgrn_level_masks())
    q_c, ff_c, fb_c, i_c = 1, 2, 3, 4
    blk = (HG_BLOCK, HG_WIDTH)
    fwd = lambda c: pl.BlockSpec(blk, lambda b: (b, c))
    bwd = lambda c: pl.BlockSpec(blk, lambda b: (nb - 1 - b, c))
    return pl.pallas_call(
        functools.partial(_hgrn_kernel, layer=layer),
        grid=(nb,),
        in_specs=[pl.BlockSpec((layers, 2, HG_WIDTH), lambda b: (0, 0, 0)),
                  pl.BlockSpec(masks.shape, lambda b: (0, 0, 0, 0)),
                  fwd(q_c), fwd(ff_c), fwd(i_c), bwd(q_c), bwd(fb_c), bwd(i_c)],
        out_specs=[pl.BlockSpec(blk, lambda b: (b, 0)),
                   pl.BlockSpec(blk, lambda b: (nb - 1 - b, 0))],
        out_shape=[jax.ShapeDtypeStruct((t, HG_WIDTH), jnp.bfloat16)] * 2,
        scratch_shapes=[pltpu.VMEM((2, HG_HEADS, HG_DIM, HG_DIM), jnp.float32)],
        compiler_params=_params("arbitrary"),
        name="hgrn2_bidirectional",
    )(lb_logits, masks, rest, rest, rest, rest, rest, rest)


def _merge_kernel(oa_ref, of_ref, ob_ref, zb_ref, g_ref, bg_ref, hgw_ref, wa_ref, wb_ref, y_ref):
    f32 = jnp.float32
    d = y_ref.shape[1]
    zb = zb_ref[...].astype(f32)
    gate_b = zb * jax.nn.sigmoid(zb)
    osum = of_ref[...].astype(f32) + ob_ref[...].astype(f32)
    parts = []
    for h in range(HG_HEADS):
        sl = slice(h * HG_DIM, (h + 1) * HG_DIM)
        oh = osum[:, sl]
        ms = jnp.mean(oh * oh, axis=-1, keepdims=True)
        parts.append(oh * lax.rsqrt(ms + NORM_EPS) * hgw_ref[:, sl])
    o_b = (jnp.concatenate(parts, axis=-1) * gate_b).astype(jnp.bfloat16)

    pa = jnp.dot(oa_ref[...], wa_ref[...], preferred_element_type=f32)
    pb = jnp.dot(o_b, wb_ref[...], preferred_element_type=f32)
    y = (jax.nn.sigmoid(g_ref[:, :d].astype(f32) + bg_ref[:, :d]) * pa
         + jax.nn.sigmoid(g_ref[:, d:].astype(f32) + bg_ref[:, d:]) * pb)
    y_ref[...] = y.astype(y_ref.dtype)


def _out_proj_kernel(x_ref, y_ref, wo_ref, pw_ref, o_ref):
    u = jnp.dot(y_ref[...], wo_ref[...], preferred_element_type=jnp.float32)
    ms = jnp.mean(u * u, axis=-1, keepdims=True)
    o_ref[...] = x_ref[...] + u * lax.rsqrt(ms + NORM_EPS) * pw_ref[...]


def _const_spec(shape, r=0, c=0):
    return pl.BlockSpec(shape, lambda i: (r, c), pipeline_mode=pl.Buffered(1))


def _merge(oa, o_f, o_b, gz, b_gate, hg_norm_w, w_br, tm=512):
    t, d = oa.shape[0], w_br.shape[1]
    tile = lambda w, c: pl.BlockSpec((tm, w), lambda i: (i, c))
    return pl.pallas_call(
        _merge_kernel,
        grid=(t // tm,),
        in_specs=[tile(NA_WIDTH, 0), tile(HG_WIDTH, 0), tile(HG_WIDTH, 0),
                  tile(HG_WIDTH, 2 * d // HG_WIDTH), tile(2 * d, 0),
                  _const_spec((1, 2 * d)), _const_spec((1, HG_WIDTH)),
                  _const_spec((NA_WIDTH, d)), _const_spec((HG_WIDTH, d), NA_WIDTH // HG_WIDTH, 0)],
        out_specs=tile(d, 0),
        out_shape=jax.ShapeDtypeStruct((t, d), jnp.bfloat16),
        compiler_params=_params("parallel"),
        name="branch_merge",
    )(oa, o_f, o_b, gz, gz, b_gate.reshape(1, 2 * d), hg_norm_w.reshape(1, HG_WIDTH), w_br, w_br)


def _out_proj(x, y, w_o, post_w, tm=512):
    t, d = x.shape
    tile = pl.BlockSpec((tm, d), lambda i: (i, 0))
    return pl.pallas_call(
        _out_proj_kernel,
        grid=(t // tm,),
        in_specs=[tile, tile, _const_spec((d, d)), _const_spec((1, d))],
        out_specs=tile,
        out_shape=jax.ShapeDtypeStruct((t, d), jnp.float32),
        compiler_params=_params("parallel"),
        name="out_proj_norm",
    )(x, y, w_o, post_w.reshape(1, d))


def kernel(x, norm_pre_w, w_in, b_gate, na_rel_bias, hg_lb_logits, hg_norm_w, w_branch, w_out, norm_post_w):
    b, t, d = x.shape
    depth = w_in.shape[0]
    bf16 = jnp.bfloat16
    outs = []
    for bi in range(b):
        xb = x[bi]
        for l in range(depth):
            xn = _rmsnorm(xb, norm_pre_w[l])
            qkv = _in_proj(xn, w_in[l], 0, QKV_COLS, bf16, "in_proj_qkv",
                           first_block_scale=LOG2_E * NA_HEAD_DIM ** -0.5, tm=2048)
            rest = _in_proj(xn, w_in[l], QKV_COLS, NA_WIDTH + 4 * HG_WIDTH, jnp.float32, "in_proj_rest")
            gz = _in_proj(xn, w_in[l], QKV_COLS + NA_WIDTH + 4 * HG_WIDTH, HG_WIDTH + 2 * d, bf16,
                          "in_proj_gates", rotate=1, tm=2048)
            oa = _neighbourhood_attention(qkv, rest, _na_bias_table(na_rel_bias[l]))
            o_f, o_b = _hgrn2(rest, hg_lb_logits.astype(jnp.float32), l)
            y = _merge(oa, o_f, o_b, gz, b_gate[l], hg_norm_w[l], w_branch[l].astype(bf16))
            xb = _out_proj(xb, y, w_out[l].astype(bf16), norm_post_w[l])
        outs.append(xb)
    return outs[0][None] if b == 1 else jnp.stack(outs, axis=0)
```

```python
import functools

import jax
import jax.numpy as jnp
import numpy as np
from jax import lax
from jax.experimental import pallas as pl
from jax.experimental.pallas import tpu as pltpu

D_MODEL = 2048
GRID_W = 64
NA_HEAD_DIM = 64
NA_WIDTH = 1024
NA_HEADS = 16
NA_WIN_ROWS = 8
NA_WIN_COLS = 16
HG_DIM = 128
HG_HEADS = 8
HG_WIDTH = 1024
NORM_EPS = 1e-6
QKV_COLS = 3 * NA_WIDTH
MASK_VALUE = -1e30
LOG2_E = 1.4426950408889634

VMEM_LIMIT_BYTES = 56 * 1024 * 1024


def _params(*sem, vmem_limit_bytes=VMEM_LIMIT_BYTES):
    return pltpu.CompilerParams(dimension_semantics=sem, vmem_limit_bytes=vmem_limit_bytes)


def _rmsnorm_kernel(x_ref, w_ref, o_ref):
    x = x_ref[...]
    ms = jnp.mean(x * x, axis=-1, keepdims=True)
    o_ref[...] = (x * lax.rsqrt(ms + NORM_EPS) * w_ref[...]).astype(o_ref.dtype)


def _rmsnorm(x, w, tm=1024):
    t, d = x.shape
    return pl.pallas_call(
        _rmsnorm_kernel,
        grid=(t // tm,),
        in_specs=[pl.BlockSpec((tm, d), lambda i: (i, 0)),
                  pl.BlockSpec((1, d), lambda i: (0, 0))],
        out_specs=pl.BlockSpec((tm, d), lambda i: (i, 0)),
        out_shape=jax.ShapeDtypeStruct((t, d), jnp.bfloat16),
        compiler_params=_params("parallel"),
        name="rmsnorm_pre",
    )(x, w.reshape(1, d))


def _in_proj_kernel(a_ref, w_ref, o_ref, wb_ref, *, first_block_scale):
    @pl.when(pl.program_id(1) == 0)
    def _():
        wb_ref[...] = w_ref[...].astype(wb_ref.dtype)

    acc = jnp.dot(a_ref[...], wb_ref[...], preferred_element_type=jnp.float32)
    if first_block_scale is not None:
        acc = acc * jnp.where(pl.program_id(0) == 0, first_block_scale, 1.0)
    o_ref[...] = acc.astype(o_ref.dtype)


def _in_proj(a, w, col0, ncols, out_dtype, name, first_block_scale=None, rotate=0, tm=1024, tn=1024):
    m, k = a.shape
    assert col0 % tn == 0 and ncols % tn == 0 and m % tm == 0
    j0 = col0 // tn
    nblk = ncols // tn
    out_blk = lambda j: jnp.where(j < rotate, j + nblk - rotate, j - rotate)
    return pl.pallas_call(
        functools.partial(_in_proj_kernel, first_block_scale=first_block_scale),
        grid=(nblk, m // tm),
        in_specs=[pl.BlockSpec((tm, k), lambda j, i: (i, 0)),
                  pl.BlockSpec((k, tn), lambda j, i: (0, j0 + j))],
        out_specs=pl.BlockSpec((tm, tn), lambda j, i: (i, out_blk(j))),
        out_shape=jax.ShapeDtypeStruct((m, ncols), out_dtype),
        scratch_shapes=[pltpu.VMEM((k, tn), a.dtype)],
        compiler_params=_params("arbitrary", "arbitrary"),
        name=name,
    )(a, w)


def _na_bias_table(rpb):
    qc = np.arange(GRID_W)[:, None]
    kc = np.arange(GRID_W)[None, :]
    cs = np.clip(qc - NA_WIN_COLS // 2, 0, GRID_W - NA_WIN_COLS)
    col_mask = (kc >= cs) & (kc < cs + NA_WIN_COLS)
    col_idx = np.clip(kc - qc + NA_WIN_COLS - 1, 0, 2 * NA_WIN_COLS - 2)
    pick = (col_idx[:, :, None] == np.arange(2 * NA_WIN_COLS - 1)) & col_mask[:, :, None]
    tz = jnp.einsum("hrc,qkc->hqrk", rpb.astype(jnp.float32) * LOG2_E, jnp.asarray(pick, jnp.float32),
                    precision=lax.Precision.HIGHEST)
    tz = tz + jnp.asarray(np.where(col_mask, 0.0, MASK_VALUE)[None, :, None, :], jnp.float32)
    band = NA_WIN_ROWS * GRID_W
    variants = [tz[:, :, NA_WIN_ROWS - 1 - var: 2 * NA_WIN_ROWS - 1 - var, :]
                .reshape(NA_HEADS, GRID_W, band) for var in range(NA_WIN_ROWS)]
    return jnp.stack(variants, axis=0)


NA_ROWS_PER_STEP = 4


def _na_kernel(q_ref, k0_ref, v0_ref, *refs, rows):
    n = NA_ROWS_PER_STEP
    kn_refs, vn_refs, z_ref = refs[:n], refs[n:2 * n], refs[2 * n]
    bias_refs = refs[2 * n + 1:3 * n + 1]
    o_ref, k_ring, v_ring, s_ref, m_ref = refs[3 * n + 1:]
    lanes = 2 * NA_HEAD_DIM
    band = NA_WIN_ROWS * GRID_W
    half = NA_WIN_ROWS // 2
    first = lax.broadcasted_iota(jnp.int32, (GRID_W, lanes), 1) < NA_HEAD_DIM
    pairs = NA_HEADS // 2
    ones = jnp.ones((band, lanes), jnp.bfloat16)

    step = pl.program_id(0)

    @pl.when(step == 0)
    def _():
        k_ring[...] = k0_ref[...]
        v_ring[...] = v0_ref[...]

    def row_state(i):
        r = step * NA_ROWS_PER_STEP + i
        enter = jnp.clip(r + half - 1, NA_WIN_ROWS - 1, rows - 1) & (NA_WIN_ROWS - 1)
        start = jnp.clip(r - half, 0, rows - NA_WIN_ROWS)
        return enter, [(start + j) & (NA_WIN_ROWS - 1) for j in range(NA_WIN_ROWS)]

    def score_phase(i, slots):
        qrows = slice(i * GRID_W, (i + 1) * GRID_W)
        for p in range(pairs):
            sl = slice(p * lanes, (p + 1) * lanes)
            qp = q_ref[qrows, sl]
            zero = jnp.zeros_like(qp)
            q_bd = jnp.concatenate([jnp.where(first, qp, zero), jnp.where(first, zero, qp)], axis=0)
            k_band = jnp.concatenate([k_ring[j, :, sl] for j in slots], axis=0)
            s = lax.dot_general(q_bd, k_band, (((1,), (1,)), ((), ())),
                                preferred_element_type=jnp.float32) + bias_refs[i][0, p]
            s_ref[i, p] = s
            m_ref[i, p] = jnp.broadcast_to(jnp.max(s, axis=-1, keepdims=True), (2 * GRID_W, lanes))

    def value_phase(i, slots):
        qrows = slice(i * GRID_W, (i + 1) * GRID_W)
        for p in range(pairs):
            sl = slice(p * lanes, (p + 1) * lanes)
            m = m_ref[i, p]
            e = jnp.exp2(s_ref[i, p] - jnp.concatenate([m] * (band // lanes), axis=-1))
            v_band = jnp.concatenate([v_ring[j, :, sl] for j in slots], axis=0)
            v_ext = jnp.concatenate([v_band, ones], axis=1)
            ol = jnp.dot(e.astype(jnp.bfloat16), v_ext, preferred_element_type=jnp.float32)
            o = ol[:, :lanes] / ol[:, lanes:]
            o = jnp.where(first, o[:GRID_W], o[GRID_W:])
            z = z_ref[qrows, sl]
            o_ref[qrows, sl] = (o * (z * jax.nn.sigmoid(z))).astype(o_ref.dtype)

    states = [row_state(i) for i in range(NA_ROWS_PER_STEP)]
    for i, (enter, slots) in enumerate(states):
        k_ring[enter] = kn_refs[i][0]
        score_phase(i, slots)
        if i > 0:
            value_phase(i - 1, states[i - 1][1])
        v_ring[enter] = vn_refs[i][0]
    value_phase(NA_ROWS_PER_STEP - 1, states[-1][1])


def _neighbourhood_attention(qkv, gates, bias_tab):
    t = qkv.shape[0]
    rows = t // GRID_W
    band = NA_WIN_ROWS * GRID_W

    def row_start(r):
        return jnp.clip(r - NA_WIN_ROWS // 2, 0, rows - NA_WIN_ROWS)

    n = NA_ROWS_PER_STEP
    assert rows % n == 0
    row_blk = (GRID_W, NA_WIDTH)
    step_blk = (n * GRID_W, NA_WIDTH)
    ring = (NA_WIN_ROWS, GRID_W, NA_WIDTH)
    half = NA_WIN_ROWS // 2
    first_rows = [pl.BlockSpec(ring, lambda s, c=c: (0, 0, c), pipeline_mode=pl.Buffered(1)) for c in (1, 2)]
    entering = [pl.BlockSpec((1,) + row_blk,
                             lambda s, c=c, i=i: (jnp.clip(s * n + i + half - 1, NA_WIN_ROWS - 1, rows - 1), 0, c))
                for c in (1, 2) for i in range(n)]
    bias = [pl.BlockSpec((1, NA_HEADS // 2, 2 * GRID_W, band),
                         lambda s, i=i: (s * n + i - row_start(s * n + i), 0, 0, 0)) for i in range(n)]
    qkv3 = qkv.reshape(rows, GRID_W, QKV_COLS)
    bias_tab = bias_tab.reshape(NA_WIN_ROWS, NA_HEADS // 2, 2 * GRID_W, band)
    return pl.pallas_call(
        functools.partial(_na_kernel, rows=rows),
        grid=(rows // n,),
        in_specs=([pl.BlockSpec(step_blk, lambda s: (s, 0))] + first_rows + entering
                  + [pl.BlockSpec(step_blk, lambda s: (s, 0))] + bias),
        out_specs=pl.BlockSpec(step_blk, lambda s: (s, 0)),
        out_shape=jax.ShapeDtypeStruct((t, NA_WIDTH), jnp.bfloat16),
        scratch_shapes=[pltpu.VMEM(ring, qkv.dtype), pltpu.VMEM(ring, qkv.dtype),
                        pltpu.VMEM((n, NA_HEADS // 2, 2 * GRID_W, band), jnp.float32),
                        pltpu.VMEM((n, NA_HEADS // 2, 2 * GRID_W, 2 * NA_HEAD_DIM), jnp.float32)],
        compiler_params=_params("arbitrary"),
        name="neighbourhood_attention",
    )(qkv, qkv3, qkv3, *([qkv3] * (2 * n)), gates, *([bias_tab] * n))


HG_CHUNK = 64
HG_BLOCK = 256


SUBLANES = 8


def _hgrn_level_masks():
    c = HG_CHUNK
    t = np.arange(c)[:, None]
    s = np.arange(c)[None, :]
    out = []
    for reverse in (False, True):
        tt, ss = (c - 1 - t, c - 1 - s) if reverse else (t, s)
        x = tt ^ ss
        levels = [tt == ss]
        m = 1
        while m < c:
            levels.append((tt > ss) & (x >= m) & (x < 2 * m))
            m *= 2
        if len(levels) % 2:
            levels.append(np.zeros((c, c), bool))
        out.append(np.stack([np.concatenate(levels[i:i + 2], axis=1) for i in range(0, len(levels), 2)]))
    return np.stack(out).astype(np.float32)


def _hgrn_chunk(q, fl, v, lb, mask_ref, reverse):
    c, sub = HG_CHUNK, SUBLANES
    nv = c // sub
    bf16 = jnp.bfloat16
    en = jnp.exp2(jnp.abs(fl) * (-LOG2_E))
    big = 1.0 / (1.0 + en)
    small = en * big
    nonneg = fl >= 0.0
    f = lb + (1.0 - lb) * jnp.where(nonneg, big, small)
    g = jnp.log2(f)
    k = (1.0 - lb) * jnp.where(nonneg, small, big)

    srow = lax.broadcasted_iota(jnp.int32, (sub, HG_DIM), 0)
    cp = (sub - 1 - srow) if reverse else srow

    def prev_shift(x, j):
        return pltpu.roll(x, (sub - j) if reverse else j, 0)

    def row_bcast(x, p):
        i = (sub - 1 - p) if reverse else p
        return jnp.broadcast_to(x[i:i + 1, :], (sub, HG_DIM))

    def rows(lst):
        return jnp.concatenate(lst[::-1] if reverse else lst, axis=0)

    blocks = range(nv - 1, -1, -1) if reverse else range(nv)
    gs = [g[b * sub:(b + 1) * sub] for b in blocks]
    loc = []
    for x in gs:
        j = 1
        while j < sub:
            x = x + jnp.where(cp >= j, prev_shift(x, j), 0.0)
            j *= 2
        loc.append(x)
    carry = [None, row_bcast(loc[0], sub - 1)]
    a = [loc[0]]
    for i in range(1, nv):
        a.append(loc[i] + carry[i])
        carry.append(carry[i] + row_bcast(loc[i], sub - 1))
    total = carry[nv]

    def level_exponents(m):
        if m >= sub:
            w = m // sub
            out = []
            for i in range(nv):
                mid = carry[(i // (2 * w)) * (2 * w) + w]
                out.append(a[i] - mid if (i // w) % 2 else mid - a[i])
            return out
        if m == 1:
            return [jnp.where((cp & 1) == 1, x, 0.0) for x in gs]
        out = []
        for x in a:
            r = row_bcast(x, m - 1)
            for blk in range(1, sub // (2 * m)):
                r = jnp.where(cp < blk * 2 * m, r, row_bcast(x, blk * 2 * m + m - 1))
            out.append(-jnp.abs(x - r))
        return out

    d = 1 if reverse else 0
    qb, kb = q.astype(bf16), k.astype(bf16)
    groups = [(qb, kb)]
    m = 1
    while m < c:
        e = jnp.exp2(rows(level_exponents(m))).astype(bf16)
        groups.append((qb * e, kb * e))
        m *= 2

    zeros = jnp.zeros((c, HG_DIM), bf16)
    scores = None
    for slab in range(0, len(groups), 2):
        qa, ka = groups[slab]
        if slab + 1 < len(groups):
            qn, kn = groups[slab + 1]
            lhs = jnp.concatenate([qa, qn], axis=1)
            rhs = jnp.concatenate([jnp.concatenate([ka, zeros], axis=1),
                                   jnp.concatenate([zeros, kn], axis=1)], axis=0)
        else:
            lhs, rhs = qa, jnp.concatenate([ka, zeros], axis=0)
        part = mask_ref[d, slab // 2] * lax.dot_general(lhs, rhs, (((1,), (1,)), ((), ())),
                                                         preferred_element_type=jnp.float32)
        scores = part if scores is None else scores + part

    q_in = (q * jnp.exp2(rows(a))).astype(bf16)
    k_out = (k * jnp.exp2(rows([total - x for x in a]))).astype(bf16)
    vb = v.astype(bf16)
    return (scores.astype(bf16), q_in, k_out, jnp.exp2(total[0:1, :]),
            jnp.concatenate([vb, vb], axis=0), v.T.astype(bf16))


def _hgrn_chunk_state(intra, st):
    scores, q_in, k_out, decay, vv, vtb = intra
    o = jnp.dot(scores, vv, preferred_element_type=jnp.float32)
    o = o + lax.dot_general(q_in, st.astype(jnp.bfloat16), (((1,), (1,)), ((), ())),
                            preferred_element_type=jnp.float32)
    st_new = st * decay + jnp.dot(vtb, k_out, preferred_element_type=jnp.float32)
    return o, st_new


def _hgrn_kernel(lbl_ref, mask_ref, qf_ref, ff_ref, vf_ref, qb_ref, fb_ref, vb_ref,
                 of_ref, ob_ref, st_ref, *, layer):
    @pl.when(pl.program_id(0) == 0)
    def _():
        st_ref[...] = jnp.zeros_like(st_ref)

    lg = lbl_ref[...]
    mx = jnp.max(lg, axis=0, keepdims=True)
    ex = jnp.exp(lg - mx)
    lb = jnp.sum(ex[: layer + 1], axis=0) / jnp.sum(ex, axis=0)

    n = HG_BLOCK // HG_CHUNK
    refs = ((qf_ref, ff_ref, vf_ref, of_ref), (qb_ref, fb_ref, vb_ref, ob_ref))
    state = {}
    pending = None

    def finish(job):
        d, h, ci, rows, cols, intra = job
        st = st_ref[d, h] if ci == 0 else state[d, h]
        o, st = _hgrn_chunk_state(intra, st)
        refs[d][3][rows, cols] = o.astype(refs[d][3].dtype)
        if ci == n - 1:
            st_ref[d, h] = st
        else:
            state[d, h] = st

    for ci in range(n):
        for h in range(HG_HEADS):
            cols = slice(h * HG_DIM, (h + 1) * HG_DIM)
            for d in (0, 1):
                blk = ci if d == 0 else n - 1 - ci
                rows = slice(blk * HG_CHUNK, (blk + 1) * HG_CHUNK)
                q_ref, f_ref, v_ref, _ = refs[d]
                intra = _hgrn_chunk(q_ref[rows, cols], f_ref[rows, cols], v_ref[rows, cols],
                                    lb[d:d + 1, cols], mask_ref, d == 1)
                if pending is not None:
                    finish(pending)
                pending = (d, h, ci, rows, cols, intra)
    finish(pending)


def _hgrn2(rest, lb_logits, layer):
    t = rest.shape[0]
    nb = t // HG_BLOCK
    layers = lb_logits.shape[0]
    masks = jnp.asarray(_hgrn_level_masks())
    q_c, ff_c, fb_c, i_c = 1, 2, 3, 4
    blk = (HG_BLOCK, HG_WIDTH)
    fwd = lambda c: pl.BlockSpec(blk, lambda b: (b, c))
    bwd = lambda c: pl.BlockSpec(blk, lambda b: (nb - 1 - b, c))
    return pl.pallas_call(
        functools.partial(_hgrn_kernel, layer=layer),
        grid=(nb,),
        in_specs=[pl.BlockSpec((layers, 2, HG_WIDTH), lambda b: (0, 0, 0)),
                  pl.BlockSpec(masks.shape, lambda b: (0, 0, 0, 0)),
                  fwd(q_c), fwd(ff_c), fwd(i_c), bwd(q_c), bwd(fb_c), bwd(i_c)],
        out_specs=[pl.BlockSpec(blk, lambda b: (b, 0)),
                   pl.BlockSpec(blk, lambda b: (nb - 1 - b, 0))],
        out_shape=[jax.ShapeDtypeStruct((t, HG_WIDTH), jnp.bfloat16)] * 2,
        scratch_shapes=[pltpu.VMEM((2, HG_HEADS, HG_DIM, HG_DIM), jnp.float32)],
        compiler_params=_params("arbitrary"),
        name="hgrn2_bidirectional",
    )(lb_logits, masks, rest, rest, rest, rest, rest, rest)


def _merge_kernel(oa_ref, of_ref, ob_ref, zb_ref, g_ref, bg_ref, hgw_ref, wa_ref, wb_ref, y_ref):
    f32 = jnp.float32
    d = y_ref.shape[1]
    zb = zb_ref[...].astype(f32)
    gate_b = zb * jax.nn.sigmoid(zb)
    osum = of_ref[...].astype(f32) + ob_ref[...].astype(f32)
    parts = []
    for h in range(HG_HEADS):
        sl = slice(h * HG_DIM, (h + 1) * HG_DIM)
        oh = osum[:, sl]
        ms = jnp.mean(oh * oh, axis=-1, keepdims=True)
        parts.append(oh * lax.rsqrt(ms + NORM_EPS) * hgw_ref[:, sl])
    o_b = (jnp.concatenate(parts, axis=-1) * gate_b).astype(jnp.bfloat16)

    pa = jnp.dot(oa_ref[...], wa_ref[...], preferred_element_type=f32)
    pb = jnp.dot(o_b, wb_ref[...], preferred_element_type=f32)
    y = (jax.nn.sigmoid(g_ref[:, :d].astype(f32) + bg_ref[:, :d]) * pa
         + jax.nn.sigmoid(g_ref[:, d:].astype(f32) + bg_ref[:, d:]) * pb)
    y_ref[...] = y.astype(y_ref.dtype)


def _out_proj_kernel(x_ref, y_ref, wo_ref, pw_ref, o_ref):
    u = jnp.dot(y_ref[...], wo_ref[...], preferred_element_type=jnp.float32)
    ms = jnp.mean(u * u, axis=-1, keepdims=True)
    o_ref[...] = x_ref[...] + u * lax.rsqrt(ms + NORM_EPS) * pw_ref[...]


def _const_spec(shape, r=0, c=0):
    return pl.BlockSpec(shape, lambda i: (r, c), pipeline_mode=pl.Buffered(1))


def _merge(oa, o_f, o_b, gz, b_gate, hg_norm_w, w_br, tm=512):
    t, d = oa.shape[0], w_br.shape[1]
    tile = lambda w, c: pl.BlockSpec((tm, w), lambda i: (i, c))
    return pl.pallas_call(
        _merge_kernel,
        grid=(t // tm,),
        in_specs=[tile(NA_WIDTH, 0), tile(HG_WIDTH, 0), tile(HG_WIDTH, 0),
                  tile(HG_WIDTH, 2 * d // HG_WIDTH), tile(2 * d, 0),
                  _const_spec((1, 2 * d)), _const_spec((1, HG_WIDTH)),
                  _const_spec((NA_WIDTH, d)), _const_spec((HG_WIDTH, d), NA_WIDTH // HG_WIDTH, 0)],
        out_specs=tile(d, 0),
        out_shape=jax.ShapeDtypeStruct((t, d), jnp.bfloat16),
        compiler_params=_params("parallel"),
        name="branch_merge",
    )(oa, o_f, o_b, gz, gz, b_gate.reshape(1, 2 * d), hg_norm_w.reshape(1, HG_WIDTH), w_br, w_br)


def _out_proj(x, y, w_o, post_w, tm=512):
    t, d = x.shape
    tile = pl.BlockSpec((tm, d), lambda i: (i, 0))
    return pl.pallas_call(
        _out_proj_kernel,
        grid=(t // tm,),
        in_specs=[tile, tile, _const_spec((d, d)), _const_spec((1, d))],
        out_specs=tile,
        out_shape=jax.ShapeDtypeStruct((t, d), jnp.float32),
        compiler_params=_params("parallel"),
        name="out_proj_norm",
    )(x, y, w_o, post_w.reshape(1, d))


def kernel(x, norm_pre_w, w_in, b_gate, na_rel_bias, hg_lb_logits, hg_norm_w, w_branch, w_out, norm_post_w):
    b, t, d = x.shape
    depth = w_in.shape[0]
    bf16 = jnp.bfloat16
    outs = []
    for bi in range(b):
        xb = x[bi]
        for l in range(depth):
            xn = _rmsnorm(xb, norm_pre_w[l])
            qkv = _in_proj(xn, w_in[l], 0, QKV_COLS, bf16, "in_proj_qkv",
                           first_block_scale=LOG2_E * NA_HEAD_DIM ** -0.5, tm=2048)
            rest = _in_proj(xn, w_in[l], QKV_COLS, NA_WIDTH + 4 * HG_WIDTH, jnp.float32, "in_proj_rest")
            gz = _in_proj(xn, w_in[l], QKV_COLS + NA_WIDTH + 4 * HG_WIDTH, HG_WIDTH + 2 * d, bf16,
                          "in_proj_gates", rotate=1, tm=2048)
            oa = _neighbourhood_attention(qkv, rest, _na_bias_table(na_rel_bias[l]))
            o_f, o_b = _hgrn2(rest, hg_lb_logits.astype(jnp.float32), l)
            y = _merge(oa, o_f, o_b, gz, b_gate[l], hg_norm_w[l], w_branch[l].astype(bf16))
            xb = _out_proj(xb, y, w_out[l].astype(bf16), norm_post_w[l])
        outs.append(xb)
    return outs[0][None] if b == 1 else jnp.stack(outs, axis=0)
```

```python
import functools

import jax
import jax.numpy as jnp
import numpy as np
from jax import lax
from jax.experimental import pallas as pl
from jax.experimental.pallas import tpu as pltpu

D_MODEL = 2048
GRID_W = 64
NA_HEAD_DIM = 64
NA_WIDTH = 1024
NA_HEADS = 16
NA_WIN_ROWS = 8
NA_WIN_COLS = 16
HG_DIM = 128
HG_HEADS = 8
HG_WIDTH = 1024
NORM_EPS = 1e-6
QKV_COLS = 3 * NA_WIDTH
MASK_VALUE = -1e30
LOG2_E = 1.4426950408889634

VMEM_LIMIT_BYTES = 56 * 1024 * 1024


def _params(*sem, vmem_limit_bytes=VMEM_LIMIT_BYTES):
    return pltpu.CompilerParams(dimension_semantics=sem, vmem_limit_bytes=vmem_limit_bytes)


def _rmsnorm_kernel(x_ref, w_ref, o_ref):
    x = x_ref[...]
    ms = jnp.mean(x * x, axis=-1, keepdims=True)
    o_ref[...] = (x * lax.rsqrt(ms + NORM_EPS) * w_ref[...]).astype(o_ref.dtype)


def _rmsnorm(x, w, tm=1024):
    t, d = x.shape
    return pl.pallas_call(
        _rmsnorm_kernel,
        grid=(t // tm,),
        in_specs=[pl.BlockSpec((tm, d), lambda i: (i, 0)),
                  pl.BlockSpec((1, d), lambda i: (0, 0))],
        out_specs=pl.BlockSpec((tm, d), lambda i: (i, 0)),
        out_shape=jax.ShapeDtypeStruct((t, d), jnp.bfloat16),
        compiler_params=_params("parallel"),
        name="rmsnorm_pre",
    )(x, w.reshape(1, d))


def _in_proj_kernel(a_ref, w_ref, o_ref, wb_ref, *, first_block_scale):
    @pl.when(pl.program_id(1) == 0)
    def _():
        wb_ref[...] = w_ref[...].astype(wb_ref.dtype)

    acc = jnp.dot(a_ref[...], wb_ref[...], preferred_element_type=jnp.float32)
    if first_block_scale is not None:
        acc = acc * jnp.where(pl.program_id(0) == 0, first_block_scale, 1.0)
    o_ref[...] = acc.astype(o_ref.dtype)


def _in_proj(a, w, col0, ncols, out_dtype, name, first_block_scale=None, rotate=0, tm=1024, tn=1024):
    m, k = a.shape
    assert col0 % tn == 0 and ncols % tn == 0 and m % tm == 0
    j0 = col0 // tn
    nblk = ncols // tn
    out_blk = lambda j: jnp.where(j < rotate, j + nblk - rotate, j - rotate)
    return pl.pallas_call(
        functools.partial(_in_proj_kernel, first_block_scale=first_block_scale),
        grid=(nblk, m // tm),
        in_specs=[pl.BlockSpec((tm, k), lambda j, i: (i, 0)),
                  pl.BlockSpec((k, tn), lambda j, i: (0, j0 + j))],
        out_specs=pl.BlockSpec((tm, tn), lambda j, i: (i, out_blk(j))),
        out_shape=jax.ShapeDtypeStruct((m, ncols), out_dtype),
        scratch_shapes=[pltpu.VMEM((k, tn), a.dtype)],
        compiler_params=_params("arbitrary", "arbitrary"),
        name=name,
    )(a, w)


def _na_bias_table(rpb):
    qc = np.arange(GRID_W)[:, None]
    kc = np.arange(GRID_W)[None, :]
    cs = np.clip(qc - NA_WIN_COLS // 2, 0, GRID_W - NA_WIN_COLS)
    col_mask = (kc >= cs) & (kc < cs + NA_WIN_COLS)
    col_idx = np.clip(kc - qc + NA_WIN_COLS - 1, 0, 2 * NA_WIN_COLS - 2)
    pick = (col_idx[:, :, None] == np.arange(2 * NA_WIN_COLS - 1)) & col_mask[:, :, None]
    tz = jnp.einsum("hrc,qkc->hqrk", rpb.astype(jnp.float32) * LOG2_E, jnp.asarray(pick, jnp.float32),
                    precision=lax.Precision.HIGHEST)
    tz = tz + jnp.asarray(np.where(col_mask, 0.0, MASK_VALUE)[None, :, None, :], jnp.float32)
    band = NA_WIN_ROWS * GRID_W
    variants = [tz[:, :, NA_WIN_ROWS - 1 - var: 2 * NA_WIN_ROWS - 1 - var, :]
                .reshape(NA_HEADS, GRID_W, band) for var in range(NA_WIN_ROWS)]
    return jnp.stack(variants, axis=0)


NA_ROWS_PER_STEP = 4


def _na_kernel(q_ref, k0_ref, v0_ref, *refs, rows):
    n = NA_ROWS_PER_STEP
    kn_refs, vn_refs, z_ref = refs[:n], refs[n:2 * n], refs[2 * n]
    bias_refs = refs[2 * n + 1:3 * n + 1]
    o_ref, k_ring, v_ring, s_ref, m_ref = refs[3 * n + 1:]
    lanes = 2 * NA_HEAD_DIM
    band = NA_WIN_ROWS * GRID_W
    half = NA_WIN_ROWS // 2
    first = lax.broadcasted_iota(jnp.int32, (GRID_W, lanes), 1) < NA_HEAD_DIM
    pairs = NA_HEADS // 2
    ones = jnp.ones((band, lanes), jnp.bfloat16)

    step = pl.program_id(0)

    @pl.when(step == 0)
    def _():
        k_ring[...] = k0_ref[...]
        v_ring[...] = v0_ref[...]

    def row_state(i):
        r = step * NA_ROWS_PER_STEP + i
        enter = jnp.clip(r + half - 1, NA_WIN_ROWS - 1, rows - 1) & (NA_WIN_ROWS - 1)
        start = jnp.clip(r - half, 0, rows - NA_WIN_ROWS)
        return enter, [(start + j) & (NA_WIN_ROWS - 1) for j in range(NA_WIN_ROWS)]

    def score_phase(i, slots):
        qrows = slice(i * GRID_W, (i + 1) * GRID_W)
        for p in range(pairs):
            sl = slice(p * lanes, (p + 1) * lanes)
            qp = q_ref[qrows, sl]
            zero = jnp.zeros_like(qp)
            q_bd = jnp.concatenate([jnp.where(first, qp, zero), jnp.where(first, zero, qp)], axis=0)
            k_band = jnp.concatenate([k_ring[j, :, sl] for j in slots], axis=0)
            s = lax.dot_general(q_bd, k_band, (((1,), (1,)), ((), ())),
                                preferred_element_type=jnp.float32) + bias_refs[i][0, p]
            s_ref[i, p] = s
            m_ref[i, p] = jnp.broadcast_to(jnp.max(s, axis=-1, keepdims=True), (2 * GRID_W, lanes))

    def value_phase(i, slots):
        qrows = slice(i * GRID_W, (i + 1) * GRID_W)
        for p in range(pairs):
            sl = slice(p * lanes, (p + 1) * lanes)
            m = m_ref[i, p]
            e = jnp.exp2(s_ref[i, p] - jnp.concatenate([m] * (band // lanes), axis=-1))
            v_band = jnp.concatenate([v_ring[j, :, sl] for j in slots], axis=0)
            v_ext = jnp.concatenate([v_band, ones], axis=1)
            ol = jnp.dot(e.astype(jnp.bfloat16), v_ext, preferred_element_type=jnp.float32)
            o = ol[:, :lanes] / ol[:, lanes:]
            o = jnp.where(first, o[:GRID_W], o[GRID_W:])
            z = z_ref[qrows, sl]
            o_ref[qrows, sl] = (o * (z * jax.nn.sigmoid(z))).astype(o_ref.dtype)

    states = [row_state(i) for i in range(NA_ROWS_PER_STEP)]
    for i, (enter, slots) in enumerate(states):
        k_ring[enter] = kn_refs[i][0]
        score_phase(i, slots)
        if i > 0:
            value_phase(i - 1, states[i - 1][1])
        v_ring[enter] = vn_refs[i][0]
    value_phase(NA_ROWS_PER_STEP - 1, states[-1][1])


def _neighbourhood_attention(qkv, gates, bias_tab):
    t = qkv.shape[0]
    rows = t // GRID_W
    band = NA_WIN_ROWS * GRID_W

    def row_start(r):
        return jnp.clip(r - NA_WIN_ROWS // 2, 0, rows - NA_WIN_ROWS)

    n = NA_ROWS_PER_STEP
    assert rows % n == 0
    row_blk = (GRID_W, NA_WIDTH)
    step_blk = (n * GRID_W, NA_WIDTH)
    ring = (NA_WIN_ROWS, GRID_W, NA_WIDTH)
    half = NA_WIN_ROWS // 2
    first_rows = [pl.BlockSpec(ring, lambda s, c=c: (0, 0, c), pipeline_mode=pl.Buffered(1)) for c in (1, 2)]
    entering = [pl.BlockSpec((1,) + row_blk,
                             lambda s, c=c, i=i: (jnp.clip(s * n + i + half - 1, NA_WIN_ROWS - 1, rows - 1), 0, c))
                for c in (1, 2) for i in range(n)]
    bias = [pl.BlockSpec((1, NA_HEADS // 2, 2 * GRID_W, band),
                         lambda s, i=i: (s * n + i - row_start(s * n + i), 0, 0, 0)) for i in range(n)]
    qkv3 = qkv.reshape(rows, GRID_W, QKV_COLS)
    bias_tab = bias_tab.reshape(NA_WIN_ROWS, NA_HEADS // 2, 2 * GRID_W, band)
    return pl.pallas_call(
        functools.partial(_na_kernel, rows=rows),
        grid=(rows // n,),
        in_specs=([pl.BlockSpec(step_blk, lambda s: (s, 0))] + first_rows + entering
                  + [pl.BlockSpec(step_blk, lambda s: (s, 0))] + bias),
        out_specs=pl.BlockSpec(step_blk, lambda s: (s, 0)),
        out_shape=jax.ShapeDtypeStruct((t, NA_WIDTH), jnp.bfloat16),
        scratch_shapes=[pltpu.VMEM(ring, qkv.dtype), pltpu.VMEM(ring, qkv.dtype),
                        pltpu.VMEM((n, NA_HEADS // 2, 2 * GRID_W, band), jnp.float32),
                        pltpu.VMEM((n, NA_HEADS // 2, 2 * GRID_W, 2 * NA_HEAD_DIM), jnp.float32)],
        compiler_params=_params("arbitrary"),
        name="neighbourhood_attention",
    )(qkv, qkv3, qkv3, *([qkv3] * (2 * n)), gates, *([bias_tab] * n))


HG_CHUNK = 64
HG_BLOCK = 256


SUBLANES = 8


def _hgrn_level_masks():
    c = HG_CHUNK
    t = np.arange(c)[:, None]
    s = np.arange(c)[None, :]
    out = []
    for reverse in (False, True):
        tt, ss = (c - 1 - t, c - 1 - s) if reverse else (t, s)
        x = tt ^ ss
        levels = [tt == ss]
        m = 1
        while m < c:
            levels.append((tt > ss) & (x >= m) & (x < 2 * m))
            m *= 2
        if len(levels) % 2:
            levels.append(np.zeros((c, c), bool))
        out.append(np.stack([np.concatenate(levels[i:i + 2], axis=1) for i in range(0, len(levels), 2)]))
    return np.stack(out).astype(np.float32)


def _hgrn_chunk(q, fl, v, lb, mask_ref, reverse):
    c, sub = HG_CHUNK, SUBLANES
    nv = c // sub
    bf16 = jnp.bfloat16
    en = jnp.exp2(jnp.abs(fl) * (-LOG2_E))
    big = 1.0 / (1.0 + en)
    small = en * big
    nonneg = fl >= 0.0
    f = lb + (1.0 - lb) * jnp.where(nonneg, big, small)
    g = jnp.log2(f)
    k = (1.0 - lb) * jnp.where(nonneg, small, big)

    srow = lax.broadcasted_iota(jnp.int32, (sub, HG_DIM), 0)
    cp = (sub - 1 - srow) if reverse else srow

    def prev_shift(x, j):
        return pltpu.roll(x, (sub - j) if reverse else j, 0)

    def row_bcast(x, p):
        i = (sub - 1 - p) if reverse else p
        return jnp.broadcast_to(x[i:i + 1, :], (sub, HG_DIM))

    def rows(lst):
        return jnp.concatenate(lst[::-1] if reverse else lst, axis=0)

    blocks = range(nv - 1, -1, -1) if reverse else range(nv)
    gs = [g[b * sub:(b + 1) * sub] for b in blocks]
    loc = []
    for x in gs:
        j = 1
        while j < sub:
            x = x + jnp.where(cp >= j, prev_shift(x, j), 0.0)
            j *= 2
        loc.append(x)
    carry = [None, row_bcast(loc[0], sub - 1)]
    a = [loc[0]]
    for i in range(1, nv):
        a.append(loc[i] + carry[i])
        carry.append(carry[i] + row_bcast(loc[i], sub - 1))
    total = carry[nv]

    def level_exponents(m):
        if m >= sub:
            w = m // sub
            out = []
            for i in range(nv):
                mid = carry[(i // (2 * w)) * (2 * w) + w]
                out.append(a[i] - mid if (i // w) % 2 else mid - a[i])
            return out
        if m == 1:
            return [jnp.where((cp & 1) == 1, x, 0.0) for x in gs]
        out = []
        for x in a:
            r = row_bcast(x, m - 1)
            for blk in range(1, sub // (2 * m)):
                r = jnp.where(cp < blk * 2 * m, r, row_bcast(x, blk * 2 * m + m - 1))
            out.append(-jnp.abs(x - r))
        return out

    d = 1 if reverse else 0
    qb, kb = q.astype(bf16), k.astype(bf16)
    groups = [(qb, kb)]
    m = 1
    while m < c:
        e = jnp.exp2(rows(level_exponents(m))).astype(bf16)
        groups.append((qb * e, kb * e))
        m *= 2

    zeros = jnp.zeros((c, HG_DIM), bf16)
    scores = None
    for slab in range(0, len(groups), 2):
        qa, ka = groups[slab]
        if slab + 1 < len(groups):
            qn, kn = groups[slab + 1]
            lhs = jnp.concatenate([qa, qn], axis=1)
            rhs = jnp.concatenate([jnp.concatenate([ka, zeros], axis=1),
                                   jnp.concatenate([zeros, kn], axis=1)], axis=0)
        else:
            lhs, rhs = qa, jnp.concatenate([ka, zeros], axis=0)
        part = mask_ref[d, slab // 2] * lax.dot_general(lhs, rhs, (((1,), (1,)), ((), ())),
                                                         preferred_element_type=jnp.float32)
        scores = part if scores is None else scores + part

    q_in = (q * jnp.exp2(rows(a))).astype(bf16)
    k_out = (k * jnp.exp2(rows([total - x for x in a]))).astype(bf16)
    vb = v.astype(bf16)
    return (scores.astype(bf16), q_in, k_out, jnp.exp2(total[0:1, :]),
            jnp.concatenate([vb, vb], axis=0), v.T.astype(bf16))


def _hgrn_chunk_state(intra, st):
    scores, q_in, k_out, decay, vv, vtb = intra
    o = jnp.dot(scores, vv, preferred_element_type=jnp.float32)
    o = o + lax.dot_general(q_in, st.astype(jnp.bfloat16), (((1,), (1,)), ((), ())),
                            preferred_element_type=jnp.float32)
    st_new = st * decay + jnp.dot(vtb, k_out, preferred_element_type=jnp.float32)
    return o, st_new


QKV_PIECE = 512


def _hgrn_kernel(lbl_ref, mask_ref, qf_ref, ff_ref, vf_ref, qb_ref, fb_ref, vb_ref, xn_ref, wqkv_ref,
                 of_ref, ob_ref, qkv_ref, st_ref, *, layer, q_scale):
    @pl.when(pl.program_id(0) == 0)
    def _():
        st_ref[...] = jnp.zeros_like(st_ref)

    lg = lbl_ref[...]
    mx = jnp.max(lg, axis=0, keepdims=True)
    ex = jnp.exp(lg - mx)
    lb = jnp.sum(ex[: layer + 1], axis=0) / jnp.sum(ex, axis=0)

    n = HG_BLOCK // HG_CHUNK
    refs = ((qf_ref, ff_ref, vf_ref, of_ref), (qb_ref, fb_ref, vb_ref, ob_ref))
    state = {}
    pending = None

    def finish(job):
        d, h, ci, rows, cols, intra = job
        st = st_ref[d, h] if ci == 0 else state[d, h]
        o, st = _hgrn_chunk_state(intra, st)
        refs[d][3][rows, cols] = o.astype(refs[d][3].dtype)
        if ci == n - 1:
            st_ref[d, h] = st
        else:
            state[d, h] = st

    def project(piece):
        cols = slice(piece * QKV_PIECE, (piece + 1) * QKV_PIECE)
        acc = jnp.dot(xn_ref[...], wqkv_ref[:, cols], preferred_element_type=jnp.float32)
        if piece * QKV_PIECE < NA_WIDTH:
            acc = acc * q_scale
        qkv_ref[:, cols] = acc.astype(qkv_ref.dtype)

    jobs = n * HG_HEADS * 2
    pieces = qkv_ref.shape[1] // QKV_PIECE
    due = {((2 * p + 1) * jobs) // (2 * pieces): p for p in range(pieces)}
    assert len(due) == pieces
    job = 0
    for ci in range(n):
        for h in range(HG_HEADS):
            cols = slice(h * HG_DIM, (h + 1) * HG_DIM)
            for d in (0, 1):
                blk = ci if d == 0 else n - 1 - ci
                rows = slice(blk * HG_CHUNK, (blk + 1) * HG_CHUNK)
                q_ref, f_ref, v_ref, _ = refs[d]
                intra = _hgrn_chunk(q_ref[rows, cols], f_ref[rows, cols], v_ref[rows, cols],
                                    lb[d:d + 1, cols], mask_ref, d == 1)
                if pending is not None:
                    finish(pending)
                pending = (d, h, ci, rows, cols, intra)
                if job in due:
                    project(due[job])
                job += 1
    finish(pending)


def _hgrn2_and_qkv(rest, lb_logits, layer, xn, w_qkv, q_scale):
    t, dm = xn.shape
    nb = t // HG_BLOCK
    layers = lb_logits.shape[0]
    masks = jnp.asarray(_hgrn_level_masks())
    q_c, ff_c, fb_c, i_c = 1, 2, 3, 4
    blk = (HG_BLOCK, HG_WIDTH)
    fwd = lambda c: pl.BlockSpec(blk, lambda b: (b, c))
    bwd = lambda c: pl.BlockSpec(blk, lambda b: (nb - 1 - b, c))
    ncols = w_qkv.shape[1]
    return pl.pallas_call(
        functools.partial(_hgrn_kernel, layer=layer, q_scale=q_scale),
        grid=(nb,),
        in_specs=[pl.BlockSpec((layers, 2, HG_WIDTH), lambda b: (0, 0, 0)),
                  pl.BlockSpec(masks.shape, lambda b: (0, 0, 0, 0)),
                  fwd(q_c), fwd(ff_c), fwd(i_c), bwd(q_c), bwd(fb_c), bwd(i_c),
                  pl.BlockSpec((HG_BLOCK, dm), lambda b: (b, 0)),
                  pl.BlockSpec((dm, ncols), lambda b: (0, 0), pipeline_mode=pl.Buffered(1))],
        out_specs=[pl.BlockSpec(blk, lambda b: (b, 0)),
                   pl.BlockSpec(blk, lambda b: (nb - 1 - b, 0)),
                   pl.BlockSpec((HG_BLOCK, ncols), lambda b: (b, 0))],
        out_shape=[jax.ShapeDtypeStruct((t, HG_WIDTH), jnp.bfloat16)] * 2
                  + [jax.ShapeDtypeStruct((t, ncols), jnp.bfloat16)],
        scratch_shapes=[pltpu.VMEM((2, HG_HEADS, HG_DIM, HG_DIM), jnp.float32)],
        compiler_params=_params("arbitrary"),
        name="hgrn2_and_qkv_proj",
    )(lb_logits, masks, rest, rest, rest, rest, rest, rest, xn, w_qkv)


def _merge_kernel(oa_ref, of_ref, ob_ref, zb_ref, g_ref, bg_ref, hgw_ref, wa_ref, wb_ref, y_ref):
    f32 = jnp.float32
    d = y_ref.shape[1]
    zb = zb_ref[...].astype(f32)
    gate_b = zb * jax.nn.sigmoid(zb)
    osum = of_ref[...].astype(f32) + ob_ref[...].astype(f32)
    parts = []
    for h in range(HG_HEADS):
        sl = slice(h * HG_DIM, (h + 1) * HG_DIM)
        oh = osum[:, sl]
        ms = jnp.mean(oh * oh, axis=-1, keepdims=True)
        parts.append(oh * lax.rsqrt(ms + NORM_EPS) * hgw_ref[:, sl])
    o_b = (jnp.concatenate(parts, axis=-1) * gate_b).astype(jnp.bfloat16)

    pa = jnp.dot(oa_ref[...], wa_ref[...], preferred_element_type=f32)
    pb = jnp.dot(o_b, wb_ref[...], preferred_element_type=f32)
    y = (jax.nn.sigmoid(g_ref[:, :d].astype(f32) + bg_ref[:, :d]) * pa
         + jax.nn.sigmoid(g_ref[:, d:].astype(f32) + bg_ref[:, d:]) * pb)
    y_ref[...] = y.astype(y_ref.dtype)


def _out_proj_kernel(x_ref, y_ref, wo_ref, pw_ref, o_ref):
    u = jnp.dot(y_ref[...], wo_ref[...], preferred_element_type=jnp.float32)
    ms = jnp.mean(u * u, axis=-1, keepdims=True)
    o_ref[...] = x_ref[...] + u * lax.rsqrt(ms + NORM_EPS) * pw_ref[...]


def _const_spec(shape, r=0, c=0):
    return pl.BlockSpec(shape, lambda i: (r, c), pipeline_mode=pl.Buffered(1))


def _merge(oa, o_f, o_b, gz, b_gate, hg_norm_w, w_br, tm=512):
    t, d = oa.shape[0], w_br.shape[1]
    tile = lambda w, c: pl.BlockSpec((tm, w), lambda i: (i, c))
    return pl.pallas_call(
        _merge_kernel,
        grid=(t // tm,),
        in_specs=[tile(NA_WIDTH, 0), tile(HG_WIDTH, 0), tile(HG_WIDTH, 0),
                  tile(HG_WIDTH, 2 * d // HG_WIDTH), tile(2 * d, 0),
                  _const_spec((1, 2 * d)), _const_spec((1, HG_WIDTH)),
                  _const_spec((NA_WIDTH, d)), _const_spec((HG_WIDTH, d), NA_WIDTH // HG_WIDTH, 0)],
        out_specs=tile(d, 0),
        out_shape=jax.ShapeDtypeStruct((t, d), jnp.bfloat16),
        compiler_params=_params("parallel"),
        name="branch_merge",
    )(oa, o_f, o_b, gz, gz, b_gate.reshape(1, 2 * d), hg_norm_w.reshape(1, HG_WIDTH), w_br, w_br)


def _out_proj(x, y, w_o, post_w, tm=512):
    t, d = x.shape
    tile = pl.BlockSpec((tm, d), lambda i: (i, 0))
    return pl.pallas_call(
        _out_proj_kernel,
        grid=(t // tm,),
        in_specs=[tile, tile, _const_spec((d, d)), _const_spec((1, d))],
        out_specs=tile,
        out_shape=jax.ShapeDtypeStruct((t, d), jnp.float32),
        compiler_params=_params("parallel"),
        name="out_proj_norm",
    )(x, y, w_o, post_w.reshape(1, d))


def kernel(x, norm_pre_w, w_in, b_gate, na_rel_bias, hg_lb_logits, hg_norm_w, w_branch, w_out, norm_post_w):
    b, t, d = x.shape
    depth = w_in.shape[0]
    bf16 = jnp.bfloat16
    outs = []
    for bi in range(b):
        xb = x[bi]
        for l in range(depth):
            xn = _rmsnorm(xb, norm_pre_w[l])
            rest = _in_proj(xn, w_in[l], QKV_COLS, NA_WIDTH + 4 * HG_WIDTH, jnp.float32, "in_proj_rest")
            gz = _in_proj(xn, w_in[l], QKV_COLS + NA_WIDTH + 4 * HG_WIDTH, HG_WIDTH + 2 * d, bf16,
                          "in_proj_gates", rotate=1, tm=2048)
            o_f, o_b, qkv = _hgrn2_and_qkv(rest, hg_lb_logits.astype(jnp.float32), l, xn,
                                           w_in[l, :, :QKV_COLS].astype(bf16), LOG2_E * NA_HEAD_DIM ** -0.5)
            oa = _neighbourhood_attention(qkv, rest, _na_bias_table(na_rel_bias[l]))
            y = _merge(oa, o_f, o_b, gz, b_gate[l], hg_norm_w[l], w_branch[l].astype(bf16))
            xb = _out_proj(xb, y, w_out[l].astype(bf16), norm_post_w[l])
        outs.append(xb)
    return outs[0][None] if b == 1 else jnp.stack(outs, axis=0)
```

```python
import functools

import jax
import jax.numpy as jnp
import numpy as np
from jax import lax
from jax.experimental import pallas as pl
from jax.experimental.pallas import tpu as pltpu

D_MODEL = 2048
GRID_W = 64
NA_HEAD_DIM = 64
NA_WIDTH = 1024
NA_HEADS = 16
NA_WIN_ROWS = 8
NA_WIN_COLS = 16
HG_DIM = 128
HG_HEADS = 8
HG_WIDTH = 1024
NORM_EPS = 1e-6
QKV_COLS = 3 * NA_WIDTH
MASK_VALUE = -1e30
LOG2_E = 1.4426950408889634

VMEM_LIMIT_BYTES = 56 * 1024 * 1024


def _params(*sem, vmem_limit_bytes=VMEM_LIMIT_BYTES):
    return pltpu.CompilerParams(dimension_semantics=sem, vmem_limit_bytes=vmem_limit_bytes)


def _rmsnorm_kernel(x_ref, w_ref, o_ref):
    x = x_ref[...]
    ms = jnp.mean(x * x, axis=-1, keepdims=True)
    o_ref[...] = (x * lax.rsqrt(ms + NORM_EPS) * w_ref[...]).astype(o_ref.dtype)


def _rmsnorm(x, w, tm=1024):
    t, d = x.shape
    return pl.pallas_call(
        _rmsnorm_kernel,
        grid=(t // tm,),
        in_specs=[pl.BlockSpec((tm, d), lambda i: (i, 0)),
                  pl.BlockSpec((1, d), lambda i: (0, 0))],
        out_specs=pl.BlockSpec((tm, d), lambda i: (i, 0)),
        out_shape=jax.ShapeDtypeStruct((t, d), jnp.bfloat16),
        compiler_params=_params("parallel"),
        name="rmsnorm_pre",
    )(x, w.reshape(1, d))


def _in_proj_kernel(a_ref, w_ref, o_ref, wb_ref, *, first_block_scale):
    @pl.when(pl.program_id(1) == 0)
    def _():
        wb_ref[...] = w_ref[...].astype(wb_ref.dtype)

    acc = jnp.dot(a_ref[...], wb_ref[...], preferred_element_type=jnp.float32)
    if first_block_scale is not None:
        acc = acc * jnp.where(pl.program_id(0) == 0, first_block_scale, 1.0)
    o_ref[...] = acc.astype(o_ref.dtype)


def _in_proj(a, w, col0, ncols, out_dtype, name, first_block_scale=None, rotate=0, tm=1024, tn=1024):
    m, k = a.shape
    assert col0 % tn == 0 and ncols % tn == 0 and m % tm == 0
    j0 = col0 // tn
    nblk = ncols // tn
    out_blk = lambda j: jnp.where(j < rotate, j + nblk - rotate, j - rotate)
    return pl.pallas_call(
        functools.partial(_in_proj_kernel, first_block_scale=first_block_scale),
        grid=(nblk, m // tm),
        in_specs=[pl.BlockSpec((tm, k), lambda j, i: (i, 0)),
                  pl.BlockSpec((k, tn), lambda j, i: (0, j0 + j))],
        out_specs=pl.BlockSpec((tm, tn), lambda j, i: (i, out_blk(j))),
        out_shape=jax.ShapeDtypeStruct((m, ncols), out_dtype),
        scratch_shapes=[pltpu.VMEM((k, tn), a.dtype)],
        compiler_params=_params("arbitrary", "arbitrary"),
        name=name,
    )(a, w)


def _na_bias_table(rpb):
    qc = np.arange(GRID_W)[:, None]
    kc = np.arange(GRID_W)[None, :]
    cs = np.clip(qc - NA_WIN_COLS // 2, 0, GRID_W - NA_WIN_COLS)
    col_mask = (kc >= cs) & (kc < cs + NA_WIN_COLS)
    col_idx = np.clip(kc - qc + NA_WIN_COLS - 1, 0, 2 * NA_WIN_COLS - 2)
    pick = (col_idx[:, :, None] == np.arange(2 * NA_WIN_COLS - 1)) & col_mask[:, :, None]
    tz = jnp.einsum("hrc,qkc->hqrk", rpb.astype(jnp.float32) * LOG2_E, jnp.asarray(pick, jnp.float32),
                    precision=lax.Precision.HIGHEST)
    tz = tz + jnp.asarray(np.where(col_mask, 0.0, MASK_VALUE)[None, :, None, :], jnp.float32)
    band = NA_WIN_ROWS * GRID_W
    variants = [tz[:, :, NA_WIN_ROWS - 1 - var: 2 * NA_WIN_ROWS - 1 - var, :]
                .reshape(NA_HEADS, GRID_W, band) for var in range(NA_WIN_ROWS)]
    return jnp.stack(variants, axis=0)


NA_ROWS_PER_STEP = 4


def _na_kernel(q_ref, k0_ref, v0_ref, *refs, rows):
    n = NA_ROWS_PER_STEP
    kn_refs, vn_refs, z_ref = refs[:n], refs[n:2 * n], refs[2 * n]
    bias_refs = refs[2 * n + 1:3 * n + 1]
    o_ref, k_ring, v_ring, s_ref, m_ref = refs[3 * n + 1:]
    lanes = 2 * NA_HEAD_DIM
    band = NA_WIN_ROWS * GRID_W
    half = NA_WIN_ROWS // 2
    first = lax.broadcasted_iota(jnp.int32, (GRID_W, lanes), 1) < NA_HEAD_DIM
    pairs = NA_HEADS // 2
    ones = jnp.ones((band, lanes), jnp.bfloat16)

    step = pl.program_id(0)

    @pl.when(step == 0)
    def _():
        k_ring[...] = k0_ref[...]
        v_ring[...] = v0_ref[...]

    def row_state(i):
        r = step * NA_ROWS_PER_STEP + i
        enter = jnp.clip(r + half - 1, NA_WIN_ROWS - 1, rows - 1) & (NA_WIN_ROWS - 1)
        start = jnp.clip(r - half, 0, rows - NA_WIN_ROWS)
        return enter, [(start + j) & (NA_WIN_ROWS - 1) for j in range(NA_WIN_ROWS)]

    def score_phase(i, slots):
        qrows = slice(i * GRID_W, (i + 1) * GRID_W)
        for p in range(pairs):
            sl = slice(p * lanes, (p + 1) * lanes)
            qp = q_ref[qrows, sl]
            zero = jnp.zeros_like(qp)
            q_bd = jnp.concatenate([jnp.where(first, qp, zero), jnp.where(first, zero, qp)], axis=0)
            k_band = jnp.concatenate([k_ring[j, :, sl] for j in slots], axis=0)
            s = lax.dot_general(q_bd, k_band, (((1,), (1,)), ((), ())),
                                preferred_element_type=jnp.float32) + bias_refs[i][0, p]
            s_ref[i, p] = s
            m_ref[i, p] = jnp.broadcast_to(jnp.max(s, axis=-1, keepdims=True), (2 * GRID_W, lanes))

    def value_phase(i, slots):
        qrows = slice(i * GRID_W, (i + 1) * GRID_W)
        for p in range(pairs):
            sl = slice(p * lanes, (p + 1) * lanes)
            m = m_ref[i, p]
            e = jnp.exp2(s_ref[i, p] - jnp.concatenate([m] * (band // lanes), axis=-1))
            v_band = jnp.concatenate([v_ring[j, :, sl] for j in slots], axis=0)
            v_ext = jnp.concatenate([v_band, ones], axis=1)
            ol = jnp.dot(e.astype(jnp.bfloat16), v_ext, preferred_element_type=jnp.float32)
            o = ol[:, :lanes] / ol[:, lanes:]
            o = jnp.where(first, o[:GRID_W], o[GRID_W:])
            z = z_ref[qrows, sl]
            o_ref[qrows, sl] = (o * (z * jax.nn.sigmoid(z))).astype(o_ref.dtype)

    states = [row_state(i) for i in range(NA_ROWS_PER_STEP)]
    for i, (enter, slots) in enumerate(states):
        k_ring[enter] = kn_refs[i][0]
        score_phase(i, slots)
        if i > 0:
            value_phase(i - 1, states[i - 1][1])
        v_ring[enter] = vn_refs[i][0]
    value_phase(NA_ROWS_PER_STEP - 1, states[-1][1])


def _neighbourhood_attention(qkv, gates, bias_tab):
    t = qkv.shape[0]
    rows = t // GRID_W
    band = NA_WIN_ROWS * GRID_W

    def row_start(r):
        return jnp.clip(r - NA_WIN_ROWS // 2, 0, rows - NA_WIN_ROWS)

    n = NA_ROWS_PER_STEP
    assert rows % n == 0
    row_blk = (GRID_W, NA_WIDTH)
    step_blk = (n * GRID_W, NA_WIDTH)
    ring = (NA_WIN_ROWS, GRID_W, NA_WIDTH)
    half = NA_WIN_ROWS // 2
    first_rows = [pl.BlockSpec(ring, lambda s, c=c: (0, 0, c), pipeline_mode=pl.Buffered(1)) for c in (1, 2)]
    entering = [pl.BlockSpec((1,) + row_blk,
                             lambda s, c=c, i=i: (jnp.clip(s * n + i + half - 1, NA_WIN_ROWS - 1, rows - 1), 0, c))
                for c in (1, 2) for i in range(n)]
    bias = [pl.BlockSpec((1, NA_HEADS // 2, 2 * GRID_W, band),
                         lambda s, i=i: (s * n + i - row_start(s * n + i), 0, 0, 0)) for i in range(n)]
    qkv3 = qkv.reshape(rows, GRID_W, QKV_COLS)
    bias_tab = bias_tab.reshape(NA_WIN_ROWS, NA_HEADS // 2, 2 * GRID_W, band)
    return pl.pallas_call(
        functools.partial(_na_kernel, rows=rows),
        grid=(rows // n,),
        in_specs=([pl.BlockSpec(step_blk, lambda s: (s, 0))] + first_rows + entering
                  + [pl.BlockSpec(step_blk, lambda s: (s, 0))] + bias),
        out_specs=pl.BlockSpec(step_blk, lambda s: (s, 0)),
        out_shape=jax.ShapeDtypeStruct((t, NA_WIDTH), jnp.bfloat16),
        scratch_shapes=[pltpu.VMEM(ring, qkv.dtype), pltpu.VMEM(ring, qkv.dtype),
                        pltpu.VMEM((n, NA_HEADS // 2, 2 * GRID_W, band), jnp.float32),
                        pltpu.VMEM((n, NA_HEADS // 2, 2 * GRID_W, 2 * NA_HEAD_DIM), jnp.float32)],
        compiler_params=_params("arbitrary"),
        name="neighbourhood_attention",
    )(qkv, qkv3, qkv3, *([qkv3] * (2 * n)), gates, *([bias_tab] * n))


HG_CHUNK = 64
HG_BLOCK = 256


SUBLANES = 8


def _hgrn_level_masks():
    c = HG_CHUNK
    t = np.arange(c)[:, None]
    s = np.arange(c)[None, :]
    out = []
    for reverse in (False, True):
        tt, ss = (c - 1 - t, c - 1 - s) if reverse else (t, s)
        x = tt ^ ss
        levels = [tt == ss]
        m = 1
        while m < c:
            levels.append((tt > ss) & (x >= m) & (x < 2 * m))
            m *= 2
        if len(levels) % 2:
            levels.append(np.zeros((c, c), bool))
        out.append(np.stack([np.concatenate(levels[i:i + 2], axis=1) for i in range(0, len(levels), 2)]))
    return np.stack(out).astype(np.float32)


def _hgrn_chunk(q, fl, v, lb, mask_ref, reverse):
    c, sub = HG_CHUNK, SUBLANES
    nv = c // sub
    bf16 = jnp.bfloat16
    en = jnp.exp2(jnp.abs(fl) * (-LOG2_E))
    big = 1.0 / (1.0 + en)
    small = en * big
    nonneg = fl >= 0.0
    f = lb + (1.0 - lb) * jnp.where(nonneg, big, small)
    g = jnp.log2(f)
    k = (1.0 - lb) * jnp.where(nonneg, small, big)

    srow = lax.broadcasted_iota(jnp.int32, (sub, HG_DIM), 0)
    cp = (sub - 1 - srow) if reverse else srow

    def prev_shift(x, j):
        return pltpu.roll(x, (sub - j) if reverse else j, 0)

    def row_bcast(x, p):
        i = (sub - 1 - p) if reverse else p
        return jnp.broadcast_to(x[i:i + 1, :], (sub, HG_DIM))

    def rows(lst):
        return jnp.concatenate(lst[::-1] if reverse else lst, axis=0)

    blocks = range(nv - 1, -1, -1) if reverse else range(nv)
    gs = [g[b * sub:(b + 1) * sub] for b in blocks]
    loc = []
    for x in gs:
        j = 1
        while j < sub:
            x = x + jnp.where(cp >= j, prev_shift(x, j), 0.0)
            j *= 2
        loc.append(x)
    carry = [None, row_bcast(loc[0], sub - 1)]
    a = [loc[0]]
    for i in range(1, nv):
        a.append(loc[i] + carry[i])
        carry.append(carry[i] + row_bcast(loc[i], sub - 1))
    total = carry[nv]

    def level_exponents(m):
        if m >= sub:
            w = m // sub
            out = []
            for i in range(nv):
                mid = carry[(i // (2 * w)) * (2 * w) + w]
                out.append(a[i] - mid if (i // w) % 2 else mid - a[i])
            return out
        if m == 1:
            return [jnp.where((cp & 1) == 1, x, 0.0) for x in gs]
        out = []
        for x in a:
            r = row_bcast(x, m - 1)
            for blk in range(1, sub // (2 * m)):
                r = jnp.where(cp < blk * 2 * m, r, row_bcast(x, blk * 2 * m + m - 1))
            out.append(-jnp.abs(x - r))
        return out

    d = 1 if reverse else 0
    qb, kb = q.astype(bf16), k.astype(bf16)
    groups = [(qb, kb)]
    m = 1
    while m < c:
        e = jnp.exp2(rows(level_exponents(m))).astype(bf16)
        groups.append((qb * e, kb * e))
        m *= 2

    zeros = jnp.zeros((c, HG_DIM), bf16)
    scores = None
    for slab in range(0, len(groups), 2):
        qa, ka = groups[slab]
        if slab + 1 < len(groups):
            qn, kn = groups[slab + 1]
            lhs = jnp.concatenate([qa, qn], axis=1)
            rhs = jnp.concatenate([jnp.concatenate([ka, zeros], axis=1),
                                   jnp.concatenate([zeros, kn], axis=1)], axis=0)
        else:
            lhs, rhs = qa, jnp.concatenate([ka, zeros], axis=0)
        part = mask_ref[d, slab // 2] * lax.dot_general(lhs, rhs, (((1,), (1,)), ((), ())),
                                                         preferred_element_type=jnp.float32)
        scores = part if scores is None else scores + part

    q_in = (q * jnp.exp2(rows(a))).astype(bf16)
    k_out = (k * jnp.exp2(rows([total - x for x in a]))).astype(bf16)
    vb = v.astype(bf16)
    return (scores.astype(bf16), q_in, k_out, jnp.exp2(total[0:1, :]),
            jnp.concatenate([vb, vb], axis=0), v.T.astype(bf16))


def _hgrn_chunk_state(intra, st):
    scores, q_in, k_out, decay, vv, vtb = intra
    o = jnp.dot(scores, vv, preferred_element_type=jnp.float32)
    o = o + lax.dot_general(q_in, st.astype(jnp.bfloat16), (((1,), (1,)), ((), ())),
                            preferred_element_type=jnp.float32)
    st_new = st * decay + jnp.dot(vtb, k_out, preferred_element_type=jnp.float32)
    return o, st_new


def _hgrn_kernel(lbl_ref, mask_ref, qf_ref, ff_ref, vf_ref, qb_ref, fb_ref, vb_ref,
                 of_ref, ob_ref, st_ref, *, layer):
    @pl.when(pl.program_id(0) == 0)
    def _():
        st_ref[...] = jnp.zeros_like(st_ref)

    lg = lbl_ref[...]
    mx = jnp.max(lg, axis=0, keepdims=True)
    ex = jnp.exp(lg - mx)
    lb = jnp.sum(ex[: layer + 1], axis=0) / jnp.sum(ex, axis=0)

    n = HG_BLOCK // HG_CHUNK
    refs = ((qf_ref, ff_ref, vf_ref, of_ref), (qb_ref, fb_ref, vb_ref, ob_ref))
    state = {}
    pending = None

    def finish(job):
        d, h, ci, rows, cols, intra = job
        st = st_ref[d, h] if ci == 0 else state[d, h]
        o, st = _hgrn_chunk_state(intra, st)
        refs[d][3][rows, cols] = o.astype(refs[d][3].dtype)
        if ci == n - 1:
            st_ref[d, h] = st
        else:
            state[d, h] = st

    for ci in range(n):
        for h in range(HG_HEADS):
            cols = slice(h * HG_DIM, (h + 1) * HG_DIM)
            for d in (0, 1):
                blk = ci if d == 0 else n - 1 - ci
                rows = slice(blk * HG_CHUNK, (blk + 1) * HG_CHUNK)
                q_ref, f_ref, v_ref, _ = refs[d]
                intra = _hgrn_chunk(q_ref[rows, cols], f_ref[rows, cols], v_ref[rows, cols],
                                    lb[d:d + 1, cols], mask_ref, d == 1)
                if pending is not None:
                    finish(pending)
                pending = (d, h, ci, rows, cols, intra)
    finish(pending)


def _hgrn2(rest, lb_logits, layer):
    t = rest.shape[0]
    nb = t // HG_BLOCK
    layers = lb_logits.shape[0]
    masks = jnp.asarray(_hgrn_level_masks())
    q_c, ff_c, fb_c, i_c = 1, 2, 3, 4
    blk = (HG_BLOCK, HG_WIDTH)
    fwd = lambda c: pl.BlockSpec(blk, lambda b: (b, c))
    bwd = lambda c: pl.BlockSpec(blk, lambda b: (nb - 1 - b, c))
    return pl.pallas_call(
        functools.partial(_hgrn_kernel, layer=layer),
        grid=(nb,),
        in_specs=[pl.BlockSpec((layers, 2, HG_WIDTH), lambda b: (0, 0, 0)),
                  pl.BlockSpec(masks.shape, lambda b: (0, 0, 0, 0)),
                  fwd(q_c), fwd(ff_c), fwd(i_c), bwd(q_c), bwd(fb_c), bwd(i_c)],
        out_specs=[pl.BlockSpec(blk, lambda b: (b, 0)),
                   pl.BlockSpec(blk, lambda b: (nb - 1 - b, 0))],
        out_shape=[jax.ShapeDtypeStruct((t, HG_WIDTH), jnp.bfloat16)] * 2,
        scratch_shapes=[pltpu.VMEM((2, HG_HEADS, HG_DIM, HG_DIM), jnp.float32)],
        compiler_params=_params("arbitrary"),
        name="hgrn2_bidirectional",
    )(lb_logits, masks, rest, rest, rest, rest, rest, rest)


def _merge_kernel(oa_ref, of_ref, ob_ref, zb_ref, g_ref, bg_ref, hgw_ref, w_ref, y_ref, wb16_ref):
    f32 = jnp.float32
    d = y_ref.shape[1]

    @pl.when(pl.program_id(0) == 0)
    def _():
        wb16_ref[...] = w_ref[...].astype(wb16_ref.dtype)

    wa_ref = wb16_ref.at[:NA_WIDTH]
    wb_ref = wb16_ref.at[NA_WIDTH:]
    zb = zb_ref[...].astype(f32)
    gate_b = zb * jax.nn.sigmoid(zb)
    osum = of_ref[...].astype(f32) + ob_ref[...].astype(f32)
    parts = []
    for h in range(HG_HEADS):
        sl = slice(h * HG_DIM, (h + 1) * HG_DIM)
        oh = osum[:, sl]
        ms = jnp.mean(oh * oh, axis=-1, keepdims=True)
        parts.append(oh * lax.rsqrt(ms + NORM_EPS) * hgw_ref[:, sl])
    o_b = (jnp.concatenate(parts, axis=-1) * gate_b).astype(jnp.bfloat16)

    pa = jnp.dot(oa_ref[...], wa_ref[...], preferred_element_type=f32)
    pb = jnp.dot(o_b, wb_ref[...], preferred_element_type=f32)
    y = (jax.nn.sigmoid(g_ref[:, :d].astype(f32) + bg_ref[:, :d]) * pa
         + jax.nn.sigmoid(g_ref[:, d:].astype(f32) + bg_ref[:, d:]) * pb)
    y_ref[...] = y.astype(y_ref.dtype)


def _out_proj_kernel(x_ref, y_ref, wo_ref, pw_ref, o_ref, wb_ref):
    @pl.when(pl.program_id(0) == 0)
    def _():
        wb_ref[...] = wo_ref[...].astype(wb_ref.dtype)

    u = jnp.dot(y_ref[...], wb_ref[...], preferred_element_type=jnp.float32)
    ms = jnp.mean(u * u, axis=-1, keepdims=True)
    o_ref[...] = x_ref[...] + u * lax.rsqrt(ms + NORM_EPS) * pw_ref[...]


def _const_spec(shape, r=0, c=0):
    return pl.BlockSpec(shape, lambda i: (r, c), pipeline_mode=pl.Buffered(1))


def _merge(oa, o_f, o_b, gz, b_gate, hg_norm_w, w_br, tm=512):
    t, d = oa.shape[0], w_br.shape[1]
    tile = lambda w, c: pl.BlockSpec((tm, w), lambda i: (i, c))
    return pl.pallas_call(
        _merge_kernel,
        grid=(t // tm,),
        in_specs=[tile(NA_WIDTH, 0), tile(HG_WIDTH, 0), tile(HG_WIDTH, 0),
                  tile(HG_WIDTH, 2 * d // HG_WIDTH), tile(2 * d, 0),
                  _const_spec((1, 2 * d)), _const_spec((1, HG_WIDTH)), _const_spec(w_br.shape)],
        out_specs=tile(d, 0),
        out_shape=jax.ShapeDtypeStruct((t, d), jnp.bfloat16),
        scratch_shapes=[pltpu.VMEM(w_br.shape, oa.dtype)],
        compiler_params=_params("arbitrary"),
        name="branch_merge",
    )(oa, o_f, o_b, gz, gz, b_gate.reshape(1, 2 * d), hg_norm_w.reshape(1, HG_WIDTH), w_br)


def _out_proj(x, y, w_o, post_w, tm=512):
    t, d = x.shape
    tile = pl.BlockSpec((tm, d), lambda i: (i, 0))
    return pl.pallas_call(
        _out_proj_kernel,
        grid=(t // tm,),
        in_specs=[tile, tile, _const_spec((d, d)), _const_spec((1, d))],
        out_specs=tile,
        out_shape=jax.ShapeDtypeStruct((t, d), jnp.float32),
        scratch_shapes=[pltpu.VMEM((d, d), y.dtype)],
        compiler_params=_params("arbitrary"),
        name="out_proj_norm",
    )(x, y, w_o, post_w.reshape(1, d))


def kernel(x, norm_pre_w, w_in, b_gate, na_rel_bias, hg_lb_logits, hg_norm_w, w_branch, w_out, norm_post_w):
    b, t, d = x.shape
    depth = w_in.shape[0]
    bf16 = jnp.bfloat16
    outs = []
    for bi in range(b):
        xb = x[bi]
        for l in range(depth):
            xn = _rmsnorm(xb, norm_pre_w[l])
            qkv = _in_proj(xn, w_in[l], 0, QKV_COLS, bf16, "in_proj_qkv",
                           first_block_scale=LOG2_E * NA_HEAD_DIM ** -0.5, tm=2048)
            rest = _in_proj(xn, w_in[l], QKV_COLS, NA_WIDTH + 4 * HG_WIDTH, jnp.float32, "in_proj_rest")
            gz = _in_proj(xn, w_in[l], QKV_COLS + NA_WIDTH + 4 * HG_WIDTH, HG_WIDTH + 2 * d, bf16,
                          "in_proj_gates", rotate=1, tm=2048)
            oa = _neighbourhood_attention(qkv, rest, _na_bias_table(na_rel_bias[l]))
            o_f, o_b = _hgrn2(rest, hg_lb_logits.astype(jnp.float32), l)
            y = _merge(oa, o_f, o_b, gz, b_gate[l], hg_norm_w[l], w_branch[l])
            xb = _out_proj(xb, y, w_out[l], norm_post_w[l])
        outs.append(xb)
    return outs[0][None] if b == 1 else jnp.stack(outs, axis=0)
```

```python
import functools

import jax
import jax.numpy as jnp
import numpy as np
from jax import lax
from jax.experimental import pallas as pl
from jax.experimental.pallas import tpu as pltpu

D_MODEL = 2048
GRID_W = 64
NA_HEAD_DIM = 64
NA_WIDTH = 1024
NA_HEADS = 16
NA_WIN_ROWS = 8
NA_WIN_COLS = 16
HG_DIM = 128
HG_HEADS = 8
HG_WIDTH = 1024
NORM_EPS = 1e-6
QKV_COLS = 3 * NA_WIDTH
MASK_VALUE = -1e30
LOG2_E = 1.4426950408889634

VMEM_LIMIT_BYTES = 56 * 1024 * 1024


def _params(*sem, vmem_limit_bytes=VMEM_LIMIT_BYTES):
    return pltpu.CompilerParams(dimension_semantics=sem, vmem_limit_bytes=vmem_limit_bytes)


def _rmsnorm_kernel(x_ref, w_ref, o_ref):
    x = x_ref[...]
    ms = jnp.mean(x * x, axis=-1, keepdims=True)
    o_ref[...] = (x * lax.rsqrt(ms + NORM_EPS) * w_ref[...]).astype(o_ref.dtype)


def _rmsnorm(x, w, tm=1024):
    t, d = x.shape
    return pl.pallas_call(
        _rmsnorm_kernel,
        grid=(t // tm,),
        in_specs=[pl.BlockSpec((tm, d), lambda i: (i, 0)),
                  pl.BlockSpec((1, d), lambda i: (0, 0))],
        out_specs=pl.BlockSpec((tm, d), lambda i: (i, 0)),
        out_shape=jax.ShapeDtypeStruct((t, d), jnp.bfloat16),
        compiler_params=_params("parallel"),
        name="rmsnorm_pre",
    )(x, w.reshape(1, d))


def _in_proj_kernel(a_ref, w_ref, o_ref, wb_ref, *, first_block_scale):
    @pl.when(pl.program_id(1) == 0)
    def _():
        wb_ref[...] = w_ref[...].astype(wb_ref.dtype)

    acc = jnp.dot(a_ref[...], wb_ref[...], preferred_element_type=jnp.float32)
    if first_block_scale is not None:
        acc = acc * jnp.where(pl.program_id(0) == 0, first_block_scale, 1.0)
    o_ref[...] = acc.astype(o_ref.dtype)


def _in_proj(a, w, col0, ncols, out_dtype, name, first_block_scale=None, rotate=0, tm=1024, tn=1024):
    m, k = a.shape
    assert col0 % tn == 0 and ncols % tn == 0 and m % tm == 0
    j0 = col0 // tn
    nblk = ncols // tn
    out_blk = lambda j: jnp.where(j < rotate, j + nblk - rotate, j - rotate)
    return pl.pallas_call(
        functools.partial(_in_proj_kernel, first_block_scale=first_block_scale),
        grid=(nblk, m // tm),
        in_specs=[pl.BlockSpec((tm, k), lambda j, i: (i, 0)),
                  pl.BlockSpec((k, tn), lambda j, i: (0, j0 + j))],
        out_specs=pl.BlockSpec((tm, tn), lambda j, i: (i, out_blk(j))),
        out_shape=jax.ShapeDtypeStruct((m, ncols), out_dtype),
        scratch_shapes=[pltpu.VMEM((k, tn), a.dtype)],
        compiler_params=_params("arbitrary", "arbitrary"),
        name=name,
    )(a, w)


LANES = 128


def _na_bias_kernel(rpb_ref, o_ref):
    var = pl.program_id(0)
    q = lax.broadcasted_iota(jnp.int32, (GRID_W, LANES), 0)
    lane = lax.broadcasted_iota(jnp.int32, (GRID_W, LANES), 1)
    k = lane & (GRID_W - 1)
    start = jnp.clip(q - NA_WIN_COLS // 2, 0, GRID_W - NA_WIN_COLS)
    inside = (k >= start) & (k < start + NA_WIN_COLS)
    left = lane < GRID_W

    def toeplitz(h, j, lane0):
        row = rpb_ref[h, pl.ds(NA_WIN_ROWS - 1 - var + j, 1), :]
        spread = jnp.broadcast_to(row, (GRID_W, LANES))
        shift = (lane0 - (NA_WIN_COLS - 1)) % LANES
        return pltpu.roll(spread, shift, 1, stride=1, stride_axis=0)

    for h in range(NA_HEADS):
        for j in range(0, NA_WIN_ROWS, 2):
            tile = jnp.where(left, toeplitz(h, j, 0), toeplitz(h, j + 1, GRID_W))
            o_ref[0, h, :, j * GRID_W:(j + 2) * GRID_W] = jnp.where(inside, tile * LOG2_E, MASK_VALUE)


def _na_bias_table(rpb):
    heads, nrow, ncol = rpb.shape
    band = NA_WIN_ROWS * GRID_W
    rpb_wide = jnp.pad(rpb.astype(jnp.float32), ((0, 0), (0, 0), (0, LANES - ncol)))
    return pl.pallas_call(
        _na_bias_kernel,
        grid=(NA_WIN_ROWS,),
        in_specs=[pl.BlockSpec((heads, nrow, LANES), lambda v: (0, 0, 0))],
        out_specs=pl.BlockSpec((1, heads, GRID_W, band), lambda v: (v, 0, 0, 0)),
        out_shape=jax.ShapeDtypeStruct((NA_WIN_ROWS, heads, GRID_W, band), jnp.float32),
        compiler_params=_params("parallel"),
        name="na_bias_table",
    )(rpb_wide)


NA_ROWS_PER_STEP = 4


def _na_kernel(q_ref, k0_ref, v0_ref, *refs, rows):
    n = NA_ROWS_PER_STEP
    kn_refs, vn_refs, z_ref = refs[:n], refs[n:2 * n], refs[2 * n]
    bias_refs = refs[2 * n + 1:3 * n + 1]
    o_ref, k_ring, v_ring, s_ref, m_ref = refs[3 * n + 1:]
    lanes = 2 * NA_HEAD_DIM
    band = NA_WIN_ROWS * GRID_W
    half = NA_WIN_ROWS // 2
    first = lax.broadcasted_iota(jnp.int32, (GRID_W, lanes), 1) < NA_HEAD_DIM
    pairs = NA_HEADS // 2
    ones = jnp.ones((band, lanes), jnp.bfloat16)

    step = pl.program_id(0)

    @pl.when(step == 0)
    def _():
        k_ring[...] = k0_ref[...]
        v_ring[...] = v0_ref[...]

    def row_state(i):
        r = step * NA_ROWS_PER_STEP + i
        enter = jnp.clip(r + half - 1, NA_WIN_ROWS - 1, rows - 1) & (NA_WIN_ROWS - 1)
        start = jnp.clip(r - half, 0, rows - NA_WIN_ROWS)
        return enter, [(start + j) & (NA_WIN_ROWS - 1) for j in range(NA_WIN_ROWS)]

    def score_phase(i, slots):
        qrows = slice(i * GRID_W, (i + 1) * GRID_W)
        for p in range(pairs):
            sl = slice(p * lanes, (p + 1) * lanes)
            qp = q_ref[qrows, sl]
            zero = jnp.zeros_like(qp)
            q_bd = jnp.concatenate([jnp.where(first, qp, zero), jnp.where(first, zero, qp)], axis=0)
            k_band = jnp.concatenate([k_ring[j, :, sl] for j in slots], axis=0)
            s = lax.dot_general(q_bd, k_band, (((1,), (1,)), ((), ())),
                                preferred_element_type=jnp.float32) + bias_refs[i][0, p]
            s_ref[i, p] = s
            m_ref[i, p] = jnp.broadcast_to(jnp.max(s, axis=-1, keepdims=True), (2 * GRID_W, lanes))

    def value_phase(i, slots):
        qrows = slice(i * GRID_W, (i + 1) * GRID_W)
        for p in range(pairs):
            sl = slice(p * lanes, (p + 1) * lanes)
            m = m_ref[i, p]
            e = jnp.exp2(s_ref[i, p] - jnp.concatenate([m] * (band // lanes), axis=-1))
            v_band = jnp.concatenate([v_ring[j, :, sl] for j in slots], axis=0)
            v_ext = jnp.concatenate([v_band, ones], axis=1)
            ol = jnp.dot(e.astype(jnp.bfloat16), v_ext, preferred_element_type=jnp.float32)
            o = ol[:, :lanes] / ol[:, lanes:]
            o = jnp.where(first, o[:GRID_W], o[GRID_W:])
            z = z_ref[qrows, sl]
            o_ref[qrows, sl] = (o * (z * jax.nn.sigmoid(z))).astype(o_ref.dtype)

    states = [row_state(i) for i in range(NA_ROWS_PER_STEP)]
    for i, (enter, slots) in enumerate(states):
        k_ring[enter] = kn_refs[i][0]
        score_phase(i, slots)
        if i > 0:
            value_phase(i - 1, states[i - 1][1])
        v_ring[enter] = vn_refs[i][0]
    value_phase(NA_ROWS_PER_STEP - 1, states[-1][1])


def _neighbourhood_attention(qkv, gates, bias_tab):
    t = qkv.shape[0]
    rows = t // GRID_W
    band = NA_WIN_ROWS * GRID_W

    def row_start(r):
        return jnp.clip(r - NA_WIN_ROWS // 2, 0, rows - NA_WIN_ROWS)

    n = NA_ROWS_PER_STEP
    assert rows % n == 0
    row_blk = (GRID_W, NA_WIDTH)
    step_blk = (n * GRID_W, NA_WIDTH)
    ring = (NA_WIN_ROWS, GRID_W, NA_WIDTH)
    half = NA_WIN_ROWS // 2
    first_rows = [pl.BlockSpec(ring, lambda s, c=c: (0, 0, c), pipeline_mode=pl.Buffered(1)) for c in (1, 2)]
    entering = [pl.BlockSpec((1,) + row_blk,
                             lambda s, c=c, i=i: (jnp.clip(s * n + i + half - 1, NA_WIN_ROWS - 1, rows - 1), 0, c))
                for c in (1, 2) for i in range(n)]
    bias = [pl.BlockSpec((1, NA_HEADS // 2, 2 * GRID_W, band),
                         lambda s, i=i: (s * n + i - row_start(s * n + i), 0, 0, 0)) for i in range(n)]
    qkv3 = qkv.reshape(rows, GRID_W, QKV_COLS)
    bias_tab = bias_tab.reshape(NA_WIN_ROWS, NA_HEADS // 2, 2 * GRID_W, band)
    return pl.pallas_call(
        functools.partial(_na_kernel, rows=rows),
        grid=(rows // n,),
        in_specs=([pl.BlockSpec(step_blk, lambda s: (s, 0))] + first_rows + entering
                  + [pl.BlockSpec(step_blk, lambda s: (s, 0))] + bias),
        out_specs=pl.BlockSpec(step_blk, lambda s: (s, 0)),
        out_shape=jax.ShapeDtypeStruct((t, NA_WIDTH), jnp.bfloat16),
        scratch_shapes=[pltpu.VMEM(ring, qkv.dtype), pltpu.VMEM(ring, qkv.dtype),
                        pltpu.VMEM((n, NA_HEADS // 2, 2 * GRID_W, band), jnp.float32),
                        pltpu.VMEM((n, NA_HEADS // 2, 2 * GRID_W, 2 * NA_HEAD_DIM), jnp.float32)],
        compiler_params=_params("arbitrary"),
        name="neighbourhood_attention",
    )(qkv, qkv3, qkv3, *([qkv3] * (2 * n)), gates, *([bias_tab] * n))


HG_CHUNK = 64
HG_BLOCK = 256


SUBLANES = 8


def _hgrn_level_masks():
    c = HG_CHUNK
    t = np.arange(c)[:, None]
    s = np.arange(c)[None, :]
    out = []
    for reverse in (False, True):
        tt, ss = (c - 1 - t, c - 1 - s) if reverse else (t, s)
        x = tt ^ ss
        levels = [tt == ss]
        m = 1
        while m < c:
            levels.append((tt > ss) & (x >= m) & (x < 2 * m))
            m *= 2
        if len(levels) % 2:
            levels.append(np.zeros((c, c), bool))
        out.append(np.stack([np.concatenate(levels[i:i + 2], axis=1) for i in range(0, len(levels), 2)]))
    return np.stack(out).astype(np.float32)


def _hgrn_chunk(q, fl, v, lb, mask_ref, reverse):
    c, sub = HG_CHUNK, SUBLANES
    nv = c // sub
    bf16 = jnp.bfloat16
    en = jnp.exp2(jnp.abs(fl) * (-LOG2_E))
    big = 1.0 / (1.0 + en)
    small = en * big
    nonneg = fl >= 0.0
    f = lb + (1.0 - lb) * jnp.where(nonneg, big, small)
    g = jnp.log2(f)
    k = (1.0 - lb) * jnp.where(nonneg, small, big)

    srow = lax.broadcasted_iota(jnp.int32, (sub, HG_DIM), 0)
    cp = (sub - 1 - srow) if reverse else srow

    def prev_shift(x, j):
        return pltpu.roll(x, (sub - j) if reverse else j, 0)

    def row_bcast(x, p):
        i = (sub - 1 - p) if reverse else p
        return jnp.broadcast_to(x[i:i + 1, :], (sub, HG_DIM))

    def rows(lst):
        return jnp.concatenate(lst[::-1] if reverse else lst, axis=0)

    blocks = range(nv - 1, -1, -1) if reverse else range(nv)
    gs = [g[b * sub:(b + 1) * sub] for b in blocks]
    loc = []
    for x in gs:
        j = 1
        while j < sub:
            x = x + jnp.where(cp >= j, prev_shift(x, j), 0.0)
            j *= 2
        loc.append(x)
    carry = [None, row_bcast(loc[0], sub - 1)]
    a = [loc[0]]
    for i in range(1, nv):
        a.append(loc[i] + carry[i])
        carry.append(carry[i] + row_bcast(loc[i], sub - 1))
    total = carry[nv]

    def level_exponents(m):
        if m >= sub:
            w = m // sub
            out = []
            for i in range(nv):
                mid = carry[(i // (2 * w)) * (2 * w) + w]
                out.append(a[i] - mid if (i // w) % 2 else mid - a[i])
            return out
        if m == 1:
            return [jnp.where((cp & 1) == 1, x, 0.0) for x in gs]
        out = []
        for x in a:
            r = row_bcast(x, m - 1)
            for blk in range(1, sub // (2 * m)):
                r = jnp.where(cp < blk * 2 * m, r, row_bcast(x, blk * 2 * m + m - 1))
            out.append(-jnp.abs(x - r))
        return out

    d = 1 if reverse else 0
    qb, kb = q.astype(bf16), k.astype(bf16)
    groups = [(qb, kb)]
    m = 1
    while m < c:
        e = jnp.exp2(rows(level_exponents(m))).astype(bf16)
        groups.append((qb * e, kb * e))
        m *= 2

    zeros = jnp.zeros((c, HG_DIM), bf16)
    scores = None
    for slab in range(0, len(groups), 2):
        qa, ka = groups[slab]
        if slab + 1 < len(groups):
            qn, kn = groups[slab + 1]
            lhs = jnp.concatenate([qa, qn], axis=1)
            rhs = jnp.concatenate([jnp.concatenate([ka, zeros], axis=1),
                                   jnp.concatenate([zeros, kn], axis=1)], axis=0)
        else:
            lhs, rhs = qa, jnp.concatenate([ka, zeros], axis=0)
        part = mask_ref[d, slab // 2] * lax.dot_general(lhs, rhs, (((1,), (1,)), ((), ())),
                                                         preferred_element_type=jnp.float32)
        scores = part if scores is None else scores + part

    q_in = (q * jnp.exp2(rows(a))).astype(bf16)
    k_out = (k * jnp.exp2(rows([total - x for x in a]))).astype(bf16)
    vb = v.astype(bf16)
    return (scores.astype(bf16), q_in, k_out, jnp.exp2(total[0:1, :]),
            jnp.concatenate([vb, vb], axis=0), v.T.astype(bf16))


def _hgrn_chunk_state(intra, st):
    scores, q_in, k_out, decay, vv, vtb = intra
    o = jnp.dot(scores, vv, preferred_element_type=jnp.float32)
    o = o + lax.dot_general(q_in, st.astype(jnp.bfloat16), (((1,), (1,)), ((), ())),
                            preferred_element_type=jnp.float32)
    st_new = st * decay + jnp.dot(vtb, k_out, preferred_element_type=jnp.float32)
    return o, st_new


def _hgrn_kernel(lbl_ref, mask_ref, qf_ref, ff_ref, vf_ref, qb_ref, fb_ref, vb_ref,
                 of_ref, ob_ref, st_ref, *, layer):
    @pl.when(pl.program_id(0) == 0)
    def _():
        st_ref[...] = jnp.zeros_like(st_ref)

    lg = lbl_ref[...]
    mx = jnp.max(lg, axis=0, keepdims=True)
    ex = jnp.exp(lg - mx)
    lb = jnp.sum(ex[: layer + 1], axis=0) / jnp.sum(ex, axis=0)

    n = HG_BLOCK // HG_CHUNK
    refs = ((qf_ref, ff_ref, vf_ref, of_ref), (qb_ref, fb_ref, vb_ref, ob_ref))
    state = {}
    pending = None

    def finish(job):
        d, h, ci, rows, cols, intra = job
        st = st_ref[d, h] if ci == 0 else state[d, h]
        o, st = _hgrn_chunk_state(intra, st)
        refs[d][3][rows, cols] = o.astype(refs[d][3].dtype)
        if ci == n - 1:
            st_ref[d, h] = st
        else:
            state[d, h] = st

    for ci in range(n):
        for h in range(HG_HEADS):
            cols = slice(h * HG_DIM, (h + 1) * HG_DIM)
            for d in (0, 1):
                blk = ci if d == 0 else n - 1 - ci
                rows = slice(blk * HG_CHUNK, (blk + 1) * HG_CHUNK)
                q_ref, f_ref, v_ref, _ = refs[d]
                intra = _hgrn_chunk(q_ref[rows, cols], f_ref[rows, cols], v_ref[rows, cols],
                                    lb[d:d + 1, cols], mask_ref, d == 1)
                if pending is not None:
                    finish(pending)
                pending = (d, h, ci, rows, cols, intra)
    finish(pending)


def _hgrn2(rest, lb_logits, layer):
    t = rest.shape[0]
    nb = t // HG_BLOCK
    layers = lb_logits.shape[0]
    masks = jnp.asarray(_hgrn_level_masks())
    q_c, ff_c, fb_c, i_c = 1, 2, 3, 4
    blk = (HG_BLOCK, HG_WIDTH)
    fwd = lambda c: pl.BlockSpec(blk, lambda b: (b, c))
    bwd = lambda c: pl.BlockSpec(blk, lambda b: (nb - 1 - b, c))
    return pl.pallas_call(
        functools.partial(_hgrn_kernel, layer=layer),
        grid=(nb,),
        in_specs=[pl.BlockSpec((layers, 2, HG_WIDTH), lambda b: (0, 0, 0)),
                  pl.BlockSpec(masks.shape, lambda b: (0, 0, 0, 0)),
                  fwd(q_c), fwd(ff_c), fwd(i_c), bwd(q_c), bwd(fb_c), bwd(i_c)],
        out_specs=[pl.BlockSpec(blk, lambda b: (b, 0)),
                   pl.BlockSpec(blk, lambda b: (nb - 1 - b, 0))],
        out_shape=[jax.ShapeDtypeStruct((t, HG_WIDTH), jnp.bfloat16)] * 2,
        scratch_shapes=[pltpu.VMEM((2, HG_HEADS, HG_DIM, HG_DIM), jnp.float32)],
        compiler_params=_params("arbitrary"),
        name="hgrn2_bidirectional",
    )(lb_logits, masks, rest, rest, rest, rest, rest, rest)


def _merge_kernel(oa_ref, of_ref, ob_ref, zb_ref, g_ref, bg_ref, hgw_ref, w_ref, y_ref, wb16_ref):
    f32 = jnp.float32
    d = y_ref.shape[1]

    @pl.when(pl.program_id(0) == 0)
    def _():
        wb16_ref[...] = w_ref[...].astype(wb16_ref.dtype)

    wa_ref = wb16_ref.at[:NA_WIDTH]
    wb_ref = wb16_ref.at[NA_WIDTH:]
    zb = zb_ref[...].astype(f32)
    gate_b = zb * jax.nn.sigmoid(zb)
    osum = of_ref[...].astype(f32) + ob_ref[...].astype(f32)
    parts = []
    for h in range(HG_HEADS):
        sl = slice(h * HG_DIM, (h + 1) * HG_DIM)
        oh = osum[:, sl]
        ms = jnp.mean(oh * oh, axis=-1, keepdims=True)
        parts.append(oh * lax.rsqrt(ms + NORM_EPS) * hgw_ref[:, sl])
    o_b = (jnp.concatenate(parts, axis=-1) * gate_b).astype(jnp.bfloat16)

    pa = jnp.dot(oa_ref[...], wa_ref[...], preferred_element_type=f32)
    pb = jnp.dot(o_b, wb_ref[...], preferred_element_type=f32)
    y = (jax.nn.sigmoid(g_ref[:, :d].astype(f32) + bg_ref[:, :d]) * pa
         + jax.nn.sigmoid(g_ref[:, d:].astype(f32) + bg_ref[:, d:]) * pb)
    y_ref[...] = y.astype(y_ref.dtype)


def _out_proj_kernel(x_ref, y_ref, wo_ref, pw_ref, o_ref, wb_ref):
    @pl.when(pl.program_id(0) == 0)
    def _():
        wb_ref[...] = wo_ref[...].astype(wb_ref.dtype)

    u = jnp.dot(y_ref[...], wb_ref[...], preferred_element_type=jnp.float32)
    ms = jnp.mean(u * u, axis=-1, keepdims=True)
    o_ref[...] = x_ref[...] + u * lax.rsqrt(ms + NORM_EPS) * pw_ref[...]


def _const_spec(shape, r=0, c=0):
    return pl.BlockSpec(shape, lambda i: (r, c), pipeline_mode=pl.Buffered(1))


def _merge(oa, o_f, o_b, gz, b_gate, hg_norm_w, w_br, tm=512):
    t, d = oa.shape[0], w_br.shape[1]
    tile = lambda w, c: pl.BlockSpec((tm, w), lambda i: (i, c))
    return pl.pallas_call(
        _merge_kernel,
        grid=(t // tm,),
        in_specs=[tile(NA_WIDTH, 0), tile(HG_WIDTH, 0), tile(HG_WIDTH, 0),
                  tile(HG_WIDTH, 2 * d // HG_WIDTH), tile(2 * d, 0),
                  _const_spec((1, 2 * d)), _const_spec((1, HG_WIDTH)), _const_spec(w_br.shape)],
        out_specs=tile(d, 0),
        out_shape=jax.ShapeDtypeStruct((t, d), jnp.bfloat16),
        scratch_shapes=[pltpu.VMEM(w_br.shape, oa.dtype)],
        compiler_params=_params("arbitrary"),
        name="branch_merge",
    )(oa, o_f, o_b, gz, gz, b_gate.reshape(1, 2 * d), hg_norm_w.reshape(1, HG_WIDTH), w_br)


def _out_proj(x, y, w_o, post_w, tm=512):
    t, d = x.shape
    tile = pl.BlockSpec((tm, d), lambda i: (i, 0))
    return pl.pallas_call(
        _out_proj_kernel,
        grid=(t // tm,),
        in_specs=[tile, tile, _const_spec((d, d)), _const_spec((1, d))],
        out_specs=tile,
        out_shape=jax.ShapeDtypeStruct((t, d), jnp.float32),
        scratch_shapes=[pltpu.VMEM((d, d), y.dtype)],
        compiler_params=_params("arbitrary"),
        name="out_proj_norm",
    )(x, y, w_o, post_w.reshape(1, d))


def kernel(x, norm_pre_w, w_in, b_gate, na_rel_bias, hg_lb_logits, hg_norm_w, w_branch, w_out, norm_post_w):
    b, t, d = x.shape
    depth = w_in.shape[0]
    bf16 = jnp.bfloat16
    outs = []
    for bi in range(b):
        xb = x[bi]
        for l in range(depth):
            xn = _rmsnorm(xb, norm_pre_w[l])
            qkv = _in_proj(xn, w_in[l], 0, QKV_COLS, bf16, "in_proj_qkv",
                           first_block_scale=LOG2_E * NA_HEAD_DIM ** -0.5, tm=2048)
            rest = _in_proj(xn, w_in[l], QKV_COLS, NA_WIDTH + 4 * HG_WIDTH, jnp.float32, "in_proj_rest")
            gz = _in_proj(xn, w_in[l], QKV_COLS + NA_WIDTH + 4 * HG_WIDTH, HG_WIDTH + 2 * d, bf16,
                          "in_proj_gates", rotate=1, tm=2048)
            oa = _neighbourhood_attention(qkv, rest, _na_bias_table(na_rel_bias[l]))
            o_f, o_b = _hgrn2(rest, hg_lb_logits.astype(jnp.float32), l)
            y = _merge(oa, o_f, o_b, gz, b_gate[l], hg_norm_w[l], w_branch[l])
            xb = _out_proj(xb, y, w_out[l], norm_post_w[l])
        outs.append(xb)
    return outs[0][None] if b == 1 else jnp.stack(outs, axis=0)
```

```python
import functools

import jax
import jax.numpy as jnp
import numpy as np
from jax import lax
from jax.experimental import pallas as pl
from jax.experimental.pallas import tpu as pltpu

D_MODEL = 2048
GRID_W = 64
NA_HEAD_DIM = 64
NA_WIDTH = 1024
NA_HEADS = 16
NA_WIN_ROWS = 8
NA_WIN_COLS = 16
HG_DIM = 128
HG_HEADS = 8
HG_WIDTH = 1024
NORM_EPS = 1e-6
QKV_COLS = 3 * NA_WIDTH
MASK_VALUE = -1e30
LOG2_E = 1.4426950408889634

VMEM_LIMIT_BYTES = 56 * 1024 * 1024


def _params(*sem, vmem_limit_bytes=VMEM_LIMIT_BYTES):
    return pltpu.CompilerParams(dimension_semantics=sem, vmem_limit_bytes=vmem_limit_bytes)


def _rmsnorm_kernel(x_ref, w_ref, o_ref):
    x = x_ref[...]
    ms = jnp.mean(x * x, axis=-1, keepdims=True)
    o_ref[...] = (x * lax.rsqrt(ms + NORM_EPS) * w_ref[...]).astype(o_ref.dtype)


def _rmsnorm(x, w, tm=1024):
    t, d = x.shape
    return pl.pallas_call(
        _rmsnorm_kernel,
        grid=(t // tm,),
        in_specs=[pl.BlockSpec((tm, d), lambda i: (i, 0)),
                  pl.BlockSpec((1, d), lambda i: (0, 0))],
        out_specs=pl.BlockSpec((tm, d), lambda i: (i, 0)),
        out_shape=jax.ShapeDtypeStruct((t, d), jnp.bfloat16),
        compiler_params=_params("parallel"),
        name="rmsnorm_pre",
    )(x, w.reshape(1, d))


def _in_proj_kernel(a_ref, w_ref, o_ref, wb_ref, *, first_block_scale):
    @pl.when(pl.program_id(1) == 0)
    def _():
        wb_ref[...] = w_ref[...].astype(wb_ref.dtype)

    acc = jnp.dot(a_ref[...], wb_ref[...], preferred_element_type=jnp.float32)
    if first_block_scale is not None:
        acc = acc * jnp.where(pl.program_id(0) == 0, first_block_scale, 1.0)
    o_ref[...] = acc.astype(o_ref.dtype)


def _in_proj(a, w, col0, ncols, out_dtype, name, first_block_scale=None, rotate=0, tm=1024, tn=1024):
    m, k = a.shape
    assert col0 % tn == 0 and ncols % tn == 0 and m % tm == 0
    j0 = col0 // tn
    nblk = ncols // tn
    out_blk = lambda j: jnp.where(j < rotate, j + nblk - rotate, j - rotate)
    return pl.pallas_call(
        functools.partial(_in_proj_kernel, first_block_scale=first_block_scale),
        grid=(nblk, m // tm),
        in_specs=[pl.BlockSpec((tm, k), lambda j, i: (i, 0)),
                  pl.BlockSpec((k, tn), lambda j, i: (0, j0 + j))],
        out_specs=pl.BlockSpec((tm, tn), lambda j, i: (i, out_blk(j))),
        out_shape=jax.ShapeDtypeStruct((m, ncols), out_dtype),
        scratch_shapes=[pltpu.VMEM((k, tn), a.dtype)],
        compiler_params=_params("arbitrary", "arbitrary"),
        name=name,
    )(a, w)


LANES = 128


def _na_bias_kernel(rpb_ref, o_ref):
    var = pl.program_id(0)
    q = lax.broadcasted_iota(jnp.int32, (GRID_W, LANES), 0)
    lane = lax.broadcasted_iota(jnp.int32, (GRID_W, LANES), 1)
    k = lane & (GRID_W - 1)
    start = jnp.clip(q - NA_WIN_COLS // 2, 0, GRID_W - NA_WIN_COLS)
    inside = (k >= start) & (k < start + NA_WIN_COLS)
    left = lane < GRID_W

    def toeplitz(h, j, lane0):
        row = rpb_ref[h, pl.ds(NA_WIN_ROWS - 1 - var + j, 1), :]
        spread = jnp.broadcast_to(row, (GRID_W, LANES))
        shift = (lane0 - (NA_WIN_COLS - 1)) % LANES
        return pltpu.roll(spread, shift, 1, stride=1, stride_axis=0)

    for h in range(NA_HEADS):
        for j in range(0, NA_WIN_ROWS, 2):
            tile = jnp.where(left, toeplitz(h, j, 0), toeplitz(h, j + 1, GRID_W))
            o_ref[0, h, :, j * GRID_W:(j + 2) * GRID_W] = jnp.where(inside, tile * LOG2_E, MASK_VALUE)


def _na_bias_table(rpb):
    heads, nrow, ncol = rpb.shape
    band = NA_WIN_ROWS * GRID_W
    rpb_wide = jnp.pad(rpb.astype(jnp.float32), ((0, 0), (0, 0), (0, LANES - ncol)))
    return pl.pallas_call(
        _na_bias_kernel,
        grid=(NA_WIN_ROWS,),
        in_specs=[pl.BlockSpec((heads, nrow, LANES), lambda v: (0, 0, 0))],
        out_specs=pl.BlockSpec((1, heads, GRID_W, band), lambda v: (v, 0, 0, 0)),
        out_shape=jax.ShapeDtypeStruct((NA_WIN_ROWS, heads, GRID_W, band), jnp.float32),
        compiler_params=_params("parallel"),
        name="na_bias_table",
    )(rpb_wide)


NA_ROWS_PER_STEP = 4


def _na_kernel(q_ref, k0_ref, v0_ref, *refs, rows):
    n = NA_ROWS_PER_STEP
    kn_refs, vn_refs, z_ref = refs[:n], refs[n:2 * n], refs[2 * n]
    bias_refs = refs[2 * n + 1:3 * n + 1]
    o_ref, k_ring, v_ring, s_ref, m_ref = refs[3 * n + 1:]
    lanes = 2 * NA_HEAD_DIM
    band = NA_WIN_ROWS * GRID_W
    half = NA_WIN_ROWS // 2
    first = lax.broadcasted_iota(jnp.int32, (GRID_W, lanes), 1) < NA_HEAD_DIM
    pairs = NA_HEADS // 2
    ones = jnp.ones((band, lanes), jnp.bfloat16)

    step = pl.program_id(0)

    @pl.when(step == 0)
    def _():
        k_ring[...] = k0_ref[...]
        v_ring[...] = v0_ref[...]

    def row_state(i):
        r = step * NA_ROWS_PER_STEP + i
        enter = jnp.clip(r + half - 1, NA_WIN_ROWS - 1, rows - 1) & (NA_WIN_ROWS - 1)
        start = jnp.clip(r - half, 0, rows - NA_WIN_ROWS)
        return enter, [(start + j) & (NA_WIN_ROWS - 1) for j in range(NA_WIN_ROWS)]

    def score_phase(i, slots):
        qrows = slice(i * GRID_W, (i + 1) * GRID_W)
        for p in range(pairs):
            sl = slice(p * lanes, (p + 1) * lanes)
            qp = q_ref[qrows, sl]
            zero = jnp.zeros_like(qp)
            q_bd = jnp.concatenate([jnp.where(first, qp, zero), jnp.where(first, zero, qp)], axis=0)
            k_band = jnp.concatenate([k_ring[j, :, sl] for j in slots], axis=0)
            s = lax.dot_general(q_bd, k_band, (((1,), (1,)), ((), ())),
                                preferred_element_type=jnp.float32) + bias_refs[i][0, p]
            s_ref[i, p] = s
            m_ref[i, p] = jnp.broadcast_to(jnp.max(s, axis=-1, keepdims=True), (2 * GRID_W, lanes))

    def value_phase(i, slots):
        qrows = slice(i * GRID_W, (i + 1) * GRID_W)
        for p in range(pairs):
            sl = slice(p * lanes, (p + 1) * lanes)
            m = m_ref[i, p]
            e = jnp.exp2(s_ref[i, p] - jnp.concatenate([m] * (band // lanes), axis=-1))
            v_band = jnp.concatenate([v_ring[j, :, sl] for j in slots], axis=0)
            v_ext = jnp.concatenate([v_band, ones], axis=1)
            ol = jnp.dot(e.astype(jnp.bfloat16), v_ext, preferred_element_type=jnp.float32)
            o = ol[:, :lanes] / ol[:, lanes:]
            o = jnp.where(first, o[:GRID_W], o[GRID_W:])
            z = z_ref[qrows, sl]
            o_ref[qrows, sl] = (o * (z * jax.nn.sigmoid(z))).astype(o_ref.dtype)

    states = [row_state(i) for i in range(NA_ROWS_PER_STEP)]
    for i, (enter, slots) in enumerate(states):
        k_ring[enter] = kn_refs[i][0]
        score_phase(i, slots)
        if i > 0:
            value_phase(i - 1, states[i - 1][1])
        v_ring[enter] = vn_refs[i][0]
    value_phase(NA_ROWS_PER_STEP - 1, states[-1][1])


def _neighbourhood_attention(qkv, gates, bias_tab):
    t = qkv.shape[0]
    rows = t // GRID_W
    band = NA_WIN_ROWS * GRID_W

    def row_start(r):
        return jnp.clip(r - NA_WIN_ROWS // 2, 0, rows - NA_WIN_ROWS)

    n = NA_ROWS_PER_STEP
    assert rows % n == 0
    row_blk = (GRID_W, NA_WIDTH)
    step_blk = (n * GRID_W, NA_WIDTH)
    ring = (NA_WIN_ROWS, GRID_W, NA_WIDTH)
    half = NA_WIN_ROWS // 2
    first_rows = [pl.BlockSpec(ring, lambda s, c=c: (0, 0, c), pipeline_mode=pl.Buffered(1)) for c in (1, 2)]
    entering = [pl.BlockSpec((1,) + row_blk,
                             lambda s, c=c, i=i: (jnp.clip(s * n + i + half - 1, NA_WIN_ROWS - 1, rows - 1), 0, c))
                for c in (1, 2) for i in range(n)]
    bias = [pl.BlockSpec((1, NA_HEADS // 2, 2 * GRID_W, band),
                         lambda s, i=i: (s * n + i - row_start(s * n + i), 0, 0, 0)) for i in range(n)]
    qkv3 = qkv.reshape(rows, GRID_W, QKV_COLS)
    bias_tab = bias_tab.reshape(NA_WIN_ROWS, NA_HEADS // 2, 2 * GRID_W, band)
    return pl.pallas_call(
        functools.partial(_na_kernel, rows=rows),
        grid=(rows // n,),
        in_specs=([pl.BlockSpec(step_blk, lambda s: (s, 0))] + first_rows + entering
                  + [pl.BlockSpec(step_blk, lambda s: (s, 0))] + bias),
        out_specs=pl.BlockSpec(step_blk, lambda s: (s, 0)),
        out_shape=jax.ShapeDtypeStruct((t, NA_WIDTH), jnp.bfloat16),
        scratch_shapes=[pltpu.VMEM(ring, qkv.dtype), pltpu.VMEM(ring, qkv.dtype),
                        pltpu.VMEM((n, NA_HEADS // 2, 2 * GRID_W, band), jnp.float32),
                        pltpu.VMEM((n, NA_HEADS // 2, 2 * GRID_W, 2 * NA_HEAD_DIM), jnp.float32)],
        compiler_params=_params("arbitrary"),
        name="neighbourhood_attention",
    )(qkv, qkv3, qkv3, *([qkv3] * (2 * n)), gates, *([bias_tab] * n))


HG_CHUNK = 64
HG_BLOCK = 128


SUBLANES = 8


def _hgrn_level_masks():
    c = HG_CHUNK
    t = np.arange(c)[:, None]
    s = np.arange(c)[None, :]
    out = []
    for reverse in (False, True):
        tt, ss = (c - 1 - t, c - 1 - s) if reverse else (t, s)
        x = tt ^ ss
        levels = [tt == ss]
        m = 1
        while m < c:
            levels.append((tt > ss) & (x >= m) & (x < 2 * m))
            m *= 2
        if len(levels) % 2:
            levels.append(np.zeros((c, c), bool))
        out.append(np.stack([np.concatenate(levels[i:i + 2], axis=1) for i in range(0, len(levels), 2)]))
    return np.stack(out).astype(np.float32)


def _hgrn_chunk(q, fl, v, lb, mask_ref, reverse):
    c, sub = HG_CHUNK, SUBLANES
    nv = c // sub
    bf16 = jnp.bfloat16
    en = jnp.exp2(jnp.abs(fl) * (-LOG2_E))
    big = 1.0 / (1.0 + en)
    small = en * big
    nonneg = fl >= 0.0
    f = lb + (1.0 - lb) * jnp.where(nonneg, big, small)
    g = jnp.log2(f)
    k = (1.0 - lb) * jnp.where(nonneg, small, big)

    srow = lax.broadcasted_iota(jnp.int32, (sub, HG_DIM), 0)
    cp = (sub - 1 - srow) if reverse else srow

    def prev_shift(x, j):
        return pltpu.roll(x, (sub - j) if reverse else j, 0)

    def row_bcast(x, p):
        i = (sub - 1 - p) if reverse else p
        return jnp.broadcast_to(x[i:i + 1, :], (sub, HG_DIM))

    def rows(lst):
        return jnp.concatenate(lst[::-1] if reverse else lst, axis=0)

    blocks = range(nv - 1, -1, -1) if reverse else range(nv)
    gs = [g[b * sub:(b + 1) * sub] for b in blocks]
    loc = []
    for x in gs:
        j = 1
        while j < sub:
            x = x + jnp.where(cp >= j, prev_shift(x, j), 0.0)
            j *= 2
        loc.append(x)
    carry = [None, row_bcast(loc[0], sub - 1)]
    a = [loc[0]]
    for i in range(1, nv):
        a.append(loc[i] + carry[i])
        carry.append(carry[i] + row_bcast(loc[i], sub - 1))
    total = carry[nv]

    def level_exponents(m):
        if m >= sub:
            w = m // sub
            out = []
            for i in range(nv):
                mid = carry[(i // (2 * w)) * (2 * w) + w]
                out.append(a[i] - mid if (i // w) % 2 else mid - a[i])
            return out
        if m == 1:
            return [jnp.where((cp & 1) == 1, x, 0.0) for x in gs]
        out = []
        for x in a:
            r = row_bcast(x, m - 1)
            for blk in range(1, sub // (2 * m)):
                r = jnp.where(cp < blk * 2 * m, r, row_bcast(x, blk * 2 * m + m - 1))
            out.append(-jnp.abs(x - r))
        return out

    d = 1 if reverse else 0
    qb, kb = q.astype(bf16), k.astype(bf16)
    groups = [(qb, kb)]
    m = 1
    while m < c:
        e = jnp.exp2(rows(level_exponents(m))).astype(bf16)
        groups.append((qb * e, kb * e))
        m *= 2

    zeros = jnp.zeros((c, HG_DIM), bf16)
    scores = None
    for slab in range(0, len(groups), 2):
        qa, ka = groups[slab]
        if slab + 1 < len(groups):
            qn, kn = groups[slab + 1]
            lhs = jnp.concatenate([qa, qn], axis=1)
            rhs = jnp.concatenate([jnp.concatenate([ka, zeros], axis=1),
                                   jnp.concatenate([zeros, kn], axis=1)], axis=0)
        else:
            lhs, rhs = qa, jnp.concatenate([ka, zeros], axis=0)
        part = mask_ref[d, slab // 2] * lax.dot_general(lhs, rhs, (((1,), (1,)), ((), ())),
                                                         preferred_element_type=jnp.float32)
        scores = part if scores is None else scores + part

    q_in = (q * jnp.exp2(rows(a))).astype(bf16)
    k_out = (k * jnp.exp2(rows([total - x for x in a]))).astype(bf16)
    vb = v.astype(bf16)
    return (scores.astype(bf16), q_in, k_out, jnp.exp2(total[0:1, :]),
            jnp.concatenate([vb, vb], axis=0), v.T.astype(bf16))


def _hgrn_chunk_state(intra, st):
    scores, q_in, k_out, decay, vv, vtb = intra
    o = jnp.dot(scores, vv, preferred_element_type=jnp.float32)
    o = o + lax.dot_general(q_in, st.astype(jnp.bfloat16), (((1,), (1,)), ((), ())),
                            preferred_element_type=jnp.float32)
    st_new = st * decay + jnp.dot(vtb, k_out, preferred_element_type=jnp.float32)
    return o, st_new


def _hgrn_kernel(lbl_ref, mask_ref, qf_ref, ff_ref, vf_ref, qb_ref, fb_ref, vb_ref,
                 of_ref, ob_ref, st_ref, *, layer):
    @pl.when(pl.program_id(0) == 0)
    def _():
        st_ref[...] = jnp.zeros_like(st_ref)

    lg = lbl_ref[...]
    mx = jnp.max(lg, axis=0, keepdims=True)
    ex = jnp.exp(lg - mx)
    lb = jnp.sum(ex[: layer + 1], axis=0) / jnp.sum(ex, axis=0)

    n = HG_BLOCK // HG_CHUNK
    refs = ((qf_ref, ff_ref, vf_ref, of_ref), (qb_ref, fb_ref, vb_ref, ob_ref))
    state = {}
    pending = None

    def finish(job):
        d, h, ci, rows, cols, intra = job
        st = st_ref[d, h] if ci == 0 else state[d, h]
        o, st = _hgrn_chunk_state(intra, st)
        refs[d][3][rows, cols] = o.astype(refs[d][3].dtype)
        if ci == n - 1:
            st_ref[d, h] = st
        else:
            state[d, h] = st

    for ci in range(n):
        for h in range(HG_HEADS):
            cols = slice(h * HG_DIM, (h + 1) * HG_DIM)
            for d in (0, 1):
                blk = ci if d == 0 else n - 1 - ci
                rows = slice(blk * HG_CHUNK, (blk + 1) * HG_CHUNK)
                q_ref, f_ref, v_ref, _ = refs[d]
                intra = _hgrn_chunk(q_ref[rows, cols], f_ref[rows, cols], v_ref[rows, cols],
                                    lb[d:d + 1, cols], mask_ref, d == 1)
                if pending is not None:
                    finish(pending)
                pending = (d, h, ci, rows, cols, intra)
    finish(pending)


def _hgrn2(rest, lb_logits, layer):
    t = rest.shape[0]
    nb = t // HG_BLOCK
    layers = lb_logits.shape[0]
    masks = jnp.asarray(_hgrn_level_masks())
    q_c, ff_c, fb_c, i_c = 1, 2, 3, 4
    blk = (HG_BLOCK, HG_WIDTH)
    fwd = lambda c: pl.BlockSpec(blk, lambda b: (b, c))
    bwd = lambda c: pl.BlockSpec(blk, lambda b: (nb - 1 - b, c))
    return pl.pallas_call(
        functools.partial(_hgrn_kernel, layer=layer),
        grid=(nb,),
        in_specs=[pl.BlockSpec((layers, 2, HG_WIDTH), lambda b: (0, 0, 0)),
                  pl.BlockSpec(masks.shape, lambda b: (0, 0, 0, 0)),
                  fwd(q_c), fwd(ff_c), fwd(i_c), bwd(q_c), bwd(fb_c), bwd(i_c)],
        out_specs=[pl.BlockSpec(blk, lambda b: (b, 0)),
                   pl.BlockSpec(blk, lambda b: (nb - 1 - b, 0))],
        out_shape=[jax.ShapeDtypeStruct((t, HG_WIDTH), jnp.bfloat16)] * 2,
        scratch_shapes=[pltpu.VMEM((2, HG_HEADS, HG_DIM, HG_DIM), jnp.float32)],
        compiler_params=_params("arbitrary"),
        name="hgrn2_bidirectional",
    )(lb_logits, masks, rest, rest, rest, rest, rest, rest)


def _merge_kernel(oa_ref, of_ref, ob_ref, zb_ref, g_ref, bg_ref, hgw_ref, w_ref, y_ref, wb16_ref):
    f32 = jnp.float32
    d = y_ref.shape[1]

    @pl.when(pl.program_id(0) == 0)
    def _():
        wb16_ref[...] = w_ref[...].astype(wb16_ref.dtype)

    wa_ref = wb16_ref.at[:NA_WIDTH]
    wb_ref = wb16_ref.at[NA_WIDTH:]
    zb = zb_ref[...].astype(f32)
    gate_b = zb * jax.nn.sigmoid(zb)
    osum = of_ref[...].astype(f32) + ob_ref[...].astype(f32)
    parts = []
    for h in range(HG_HEADS):
        sl = slice(h * HG_DIM, (h + 1) * HG_DIM)
        oh = osum[:, sl]
        ms = jnp.mean(oh * oh, axis=-1, keepdims=True)
        parts.append(oh * lax.rsqrt(ms + NORM_EPS) * hgw_ref[:, sl])
    o_b = (jnp.concatenate(parts, axis=-1) * gate_b).astype(jnp.bfloat16)

    pa = jnp.dot(oa_ref[...], wa_ref[...], preferred_element_type=f32)
    pb = jnp.dot(o_b, wb_ref[...], preferred_element_type=f32)
    y = (jax.nn.sigmoid(g_ref[:, :d].astype(f32) + bg_ref[:, :d]) * pa
         + jax.nn.sigmoid(g_ref[:, d:].astype(f32) + bg_ref[:, d:]) * pb)
    y_ref[...] = y.astype(y_ref.dtype)


def _out_proj_kernel(x_ref, y_ref, wo_ref, pw_ref, o_ref, wb_ref):
    @pl.when(pl.program_id(0) == 0)
    def _():
        wb_ref[...] = wo_ref[...].astype(wb_ref.dtype)

    u = jnp.dot(y_ref[...], wb_ref[...], preferred_element_type=jnp.float32)
    ms = jnp.mean(u * u, axis=-1, keepdims=True)
    o_ref[...] = x_ref[...] + u * lax.rsqrt(ms + NORM_EPS) * pw_ref[...]


def _const_spec(shape, r=0, c=0):
    return pl.BlockSpec(shape, lambda i: (r, c), pipeline_mode=pl.Buffered(1))


def _merge(oa, o_f, o_b, gz, b_gate, hg_norm_w, w_br, tm=512):
    t, d = oa.shape[0], w_br.shape[1]
    tile = lambda w, c: pl.BlockSpec((tm, w), lambda i: (i, c))
    return pl.pallas_call(
        _merge_kernel,
        grid=(t // tm,),
        in_specs=[tile(NA_WIDTH, 0), tile(HG_WIDTH, 0), tile(HG_WIDTH, 0),
                  tile(HG_WIDTH, 2 * d // HG_WIDTH), tile(2 * d, 0),
                  _const_spec((1, 2 * d)), _const_spec((1, HG_WIDTH)), _const_spec(w_br.shape)],
        out_specs=tile(d, 0),
        out_shape=jax.ShapeDtypeStruct((t, d), jnp.bfloat16),
        scratch_shapes=[pltpu.VMEM(w_br.shape, oa.dtype)],
        compiler_params=_params("arbitrary"),
        name="branch_merge",
    )(oa, o_f, o_b, gz, gz, b_gate.reshape(1, 2 * d), hg_norm_w.reshape(1, HG_WIDTH), w_br)


def _out_proj(x, y, w_o, post_w, tm=512):
    t, d = x.shape
    tile = pl.BlockSpec((tm, d), lambda i: (i, 0))
    return pl.pallas_call(
        _out_proj_kernel,
        grid=(t // tm,),
        in_specs=[tile, tile, _const_spec((d, d)), _const_spec((1, d))],
        out_specs=tile,
        out_shape=jax.ShapeDtypeStruct((t, d), jnp.float32),
        scratch_shapes=[pltpu.VMEM((d, d), y.dtype)],
        compiler_params=_params("arbitrary"),
        name="out_proj_norm",
    )(x, y, w_o, post_w.reshape(1, d))


def kernel(x, norm_pre_w, w_in, b_gate, na_rel_bias, hg_lb_logits, hg_norm_w, w_branch, w_out, norm_post_w):
    b, t, d = x.shape
    depth = w_in.shape[0]
    bf16 = jnp.bfloat16
    outs = []
    for bi in range(b):
        xb = x[bi]
        for l in range(depth):
            xn = _rmsnorm(xb, norm_pre_w[l])
            qkv = _in_proj(xn, w_in[l], 0, QKV_COLS, bf16, "in_proj_qkv",
                           first_block_scale=LOG2_E * NA_HEAD_DIM ** -0.5, tm=2048)
            rest = _in_proj(xn, w_in[l], QKV_COLS, NA_WIDTH + 4 * HG_WIDTH, jnp.float32, "in_proj_rest")
            gz = _in_proj(xn, w_in[l], QKV_COLS + NA_WIDTH + 4 * HG_WIDTH, HG_WIDTH + 2 * d, bf16,
                          "in_proj_gates", rotate=1, tm=2048)
            oa = _neighbourhood_attention(qkv, rest, _na_bias_table(na_rel_bias[l]))
            o_f, o_b = _hgrn2(rest, hg_lb_logits.astype(jnp.float32), l)
            y = _merge(oa, o_f, o_b, gz, b_gate[l], hg_norm_w[l], w_branch[l])
            xb = _out_proj(xb, y, w_out[l], norm_post_w[l])
        outs.append(xb)
    return outs[0][None] if b == 1 else jnp.stack(outs, axis=0)
```

```python
import functools

import jax
import jax.numpy as jnp
import numpy as np
from jax import lax
from jax.experimental import pallas as pl
from jax.experimental.pallas import tpu as pltpu

D_MODEL = 2048
GRID_W = 64
NA_HEAD_DIM = 64
NA_WIDTH = 1024
NA_HEADS = 16
NA_WIN_ROWS = 8
NA_WIN_COLS = 16
HG_DIM = 128
HG_HEADS = 8
HG_WIDTH = 1024
NORM_EPS = 1e-6
QKV_COLS = 3 * NA_WIDTH
MASK_VALUE = -1e30
LOG2_E = 1.4426950408889634

VMEM_LIMIT_BYTES = 56 * 1024 * 1024


def _params(*sem, vmem_limit_bytes=VMEM_LIMIT_BYTES):
    return pltpu.CompilerParams(dimension_semantics=sem, vmem_limit_bytes=vmem_limit_bytes)


def _rmsnorm_kernel(x_ref, w_ref, o_ref):
    x = x_ref[...]
    ms = jnp.mean(x * x, axis=-1, keepdims=True)
    o_ref[...] = (x * lax.rsqrt(ms + NORM_EPS) * w_ref[...]).astype(o_ref.dtype)


def _rmsnorm(x, w, tm=1024):
    t, d = x.shape
    return pl.pallas_call(
        _rmsnorm_kernel,
        grid=(t // tm,),
        in_specs=[pl.BlockSpec((tm, d), lambda i: (i, 0)),
                  pl.BlockSpec((1, d), lambda i: (0, 0))],
        out_specs=pl.BlockSpec((tm, d), lambda i: (i, 0)),
        out_shape=jax.ShapeDtypeStruct((t, d), jnp.bfloat16),
        compiler_params=_params("parallel"),
        name="rmsnorm_pre",
    )(x, w.reshape(1, d))


def _in_proj_kernel(a_ref, w_ref, o_ref, wb_ref, *, first_block_scale):
    @pl.when(pl.program_id(1) == 0)
    def _():
        wb_ref[...] = w_ref[...].astype(wb_ref.dtype)

    acc = jnp.dot(a_ref[...], wb_ref[...], preferred_element_type=jnp.float32)
    if first_block_scale is not None:
        acc = acc * jnp.where(pl.program_id(0) == 0, first_block_scale, 1.0)
    o_ref[...] = acc.astype(o_ref.dtype)


def _in_proj(a, w, col0, ncols, out_dtype, name, first_block_scale=None, rotate=0, tm=1024, tn=1024):
    m, k = a.shape
    assert col0 % tn == 0 and ncols % tn == 0 and m % tm == 0
    j0 = col0 // tn
    nblk = ncols // tn
    out_blk = lambda j: jnp.where(j < rotate, j + nblk - rotate, j - rotate)
    return pl.pallas_call(
        functools.partial(_in_proj_kernel, first_block_scale=first_block_scale),
        grid=(nblk, m // tm),
        in_specs=[pl.BlockSpec((tm, k), lambda j, i: (i, 0)),
                  pl.BlockSpec((k, tn), lambda j, i: (0, j0 + j))],
        out_specs=pl.BlockSpec((tm, tn), lambda j, i: (i, out_blk(j))),
        out_shape=jax.ShapeDtypeStruct((m, ncols), out_dtype),
        scratch_shapes=[pltpu.VMEM((k, tn), a.dtype)],
        compiler_params=_params("arbitrary", "arbitrary"),
        name=name,
    )(a, w)


LANES = 128


def _na_bias_kernel(rpb_ref, o_ref):
    var = pl.program_id(0)
    q = lax.broadcasted_iota(jnp.int32, (GRID_W, LANES), 0)
    lane = lax.broadcasted_iota(jnp.int32, (GRID_W, LANES), 1)
    k = lane & (GRID_W - 1)
    start = jnp.clip(q - NA_WIN_COLS // 2, 0, GRID_W - NA_WIN_COLS)
    inside = (k >= start) & (k < start + NA_WIN_COLS)
    left = lane < GRID_W

    def toeplitz(h, j, lane0):
        row = rpb_ref[h, pl.ds(NA_WIN_ROWS - 1 - var + j, 1), :]
        spread = jnp.broadcast_to(row, (GRID_W, LANES))
        shift = (lane0 - (NA_WIN_COLS - 1)) % LANES
        return pltpu.roll(spread, shift, 1, stride=1, stride_axis=0)

    for h in range(NA_HEADS):
        for j in range(0, NA_WIN_ROWS, 2):
            tile = jnp.where(left, toeplitz(h, j, 0), toeplitz(h, j + 1, GRID_W))
            o_ref[0, h, :, j * GRID_W:(j + 2) * GRID_W] = jnp.where(inside, tile * LOG2_E, MASK_VALUE)


def _na_bias_table(rpb):
    heads, nrow, ncol = rpb.shape
    band = NA_WIN_ROWS * GRID_W
    rpb_wide = jnp.pad(rpb.astype(jnp.float32), ((0, 0), (0, 0), (0, LANES - ncol)))
    return pl.pallas_call(
        _na_bias_kernel,
        grid=(NA_WIN_ROWS,),
        in_specs=[pl.BlockSpec((heads, nrow, LANES), lambda v: (0, 0, 0))],
        out_specs=pl.BlockSpec((1, heads, GRID_W, band), lambda v: (v, 0, 0, 0)),
        out_shape=jax.ShapeDtypeStruct((NA_WIN_ROWS, heads, GRID_W, band), jnp.float32),
        compiler_params=_params("parallel"),
        name="na_bias_table",
    )(rpb_wide)


NA_ROWS_PER_STEP = 4


def _na_kernel(q_ref, k0_ref, v0_ref, *refs, rows):
    n = NA_ROWS_PER_STEP
    kn_refs, vn_refs, z_ref = refs[:n], refs[n:2 * n], refs[2 * n]
    bias_refs = refs[2 * n + 1:3 * n + 1]
    o_ref, k_ring, v_ring, s_ref, m_ref = refs[3 * n + 1:]
    lanes = 2 * NA_HEAD_DIM
    band = NA_WIN_ROWS * GRID_W
    half = NA_WIN_ROWS // 2
    first = lax.broadcasted_iota(jnp.int32, (GRID_W, lanes), 1) < NA_HEAD_DIM
    pairs = NA_HEADS // 2
    ones = jnp.ones((band, lanes), jnp.bfloat16)

    step = pl.program_id(0)

    @pl.when(step == 0)
    def _():
        k_ring[...] = k0_ref[...]
        v_ring[...] = v0_ref[...]

    def row_state(i):
        r = step * NA_ROWS_PER_STEP + i
        enter = jnp.clip(r + half - 1, NA_WIN_ROWS - 1, rows - 1) & (NA_WIN_ROWS - 1)
        start = jnp.clip(r - half, 0, rows - NA_WIN_ROWS)
        return enter, [(start + j) & (NA_WIN_ROWS - 1) for j in range(NA_WIN_ROWS)]

    def score_phase(i, slots):
        qrows = slice(i * GRID_W, (i + 1) * GRID_W)
        for p in range(pairs):
            sl = slice(p * lanes, (p + 1) * lanes)
            qp = q_ref[qrows, sl]
            zero = jnp.zeros_like(qp)
            q_bd = jnp.concatenate([jnp.where(first, qp, zero), jnp.where(first, zero, qp)], axis=0)
            k_band = jnp.concatenate([k_ring[j, :, sl] for j in slots], axis=0)
            m = None
            for c0 in range(0, band, 2 * lanes):
                cs = slice(c0, c0 + 2 * lanes)
                s = lax.dot_general(q_bd, k_band[cs], (((1,), (1,)), ((), ())),
                                    preferred_element_type=jnp.float32) + bias_refs[i][0, p, :, cs]
                s_ref[i, p, :, cs] = s
                mh = jnp.max(s, axis=-1, keepdims=True)
                m = mh if m is None else jnp.maximum(m, mh)
            m_ref[i, p] = jnp.broadcast_to(m, (2 * GRID_W, lanes))

    def value_phase(i, slots):
        qrows = slice(i * GRID_W, (i + 1) * GRID_W)
        for p in range(pairs):
            sl = slice(p * lanes, (p + 1) * lanes)
            m = m_ref[i, p]
            e = jnp.exp2(s_ref[i, p] - jnp.concatenate([m] * (band // lanes), axis=-1))
            v_band = jnp.concatenate([v_ring[j, :, sl] for j in slots], axis=0)
            v_ext = jnp.concatenate([v_band, ones], axis=1)
            ol = jnp.dot(e.astype(jnp.bfloat16), v_ext, preferred_element_type=jnp.float32)
            o = ol[:, :lanes] / ol[:, lanes:]
            o = jnp.where(first, o[:GRID_W], o[GRID_W:])
            z = z_ref[qrows, sl]
            o_ref[qrows, sl] = (o * (z * jax.nn.sigmoid(z))).astype(o_ref.dtype)

    states = [row_state(i) for i in range(NA_ROWS_PER_STEP)]
    for i, (enter, slots) in enumerate(states):
        k_ring[enter] = kn_refs[i][0]
        score_phase(i, slots)
        if i > 0:
            value_phase(i - 1, states[i - 1][1])
        v_ring[enter] = vn_refs[i][0]
    value_phase(NA_ROWS_PER_STEP - 1, states[-1][1])


def _neighbourhood_attention(qkv, gates, bias_tab):
    t = qkv.shape[0]
    rows = t // GRID_W
    band = NA_WIN_ROWS * GRID_W

    def row_start(r):
        return jnp.clip(r - NA_WIN_ROWS // 2, 0, rows - NA_WIN_ROWS)

    n = NA_ROWS_PER_STEP
    assert rows % n == 0
    row_blk = (GRID_W, NA_WIDTH)
    step_blk = (n * GRID_W, NA_WIDTH)
    ring = (NA_WIN_ROWS, GRID_W, NA_WIDTH)
    half = NA_WIN_ROWS // 2
    first_rows = [pl.BlockSpec(ring, lambda s, c=c: (0, 0, c), pipeline_mode=pl.Buffered(1)) for c in (1, 2)]
    entering = [pl.BlockSpec((1,) + row_blk,
                             lambda s, c=c, i=i: (jnp.clip(s * n + i + half - 1, NA_WIN_ROWS - 1, rows - 1), 0, c))
                for c in (1, 2) for i in range(n)]
    bias = [pl.BlockSpec((1, NA_HEADS // 2, 2 * GRID_W, band),
                         lambda s, i=i: (s * n + i - row_start(s * n + i), 0, 0, 0)) for i in range(n)]
    qkv3 = qkv.reshape(rows, GRID_W, QKV_COLS)
    bias_tab = bias_tab.reshape(NA_WIN_ROWS, NA_HEADS // 2, 2 * GRID_W, band)
    return pl.pallas_call(
        functools.partial(_na_kernel, rows=rows),
        grid=(rows // n,),
        in_specs=([pl.BlockSpec(step_blk, lambda s: (s, 0))] + first_rows + entering
                  + [pl.BlockSpec(step_blk, lambda s: (s, 0))] + bias),
        out_specs=pl.BlockSpec(step_blk, lambda s: (s, 0)),
        out_shape=jax.ShapeDtypeStruct((t, NA_WIDTH), jnp.bfloat16),
        scratch_shapes=[pltpu.VMEM(ring, qkv.dtype), pltpu.VMEM(ring, qkv.dtype),
                        pltpu.VMEM((n, NA_HEADS // 2, 2 * GRID_W, band), jnp.float32),
                        pltpu.VMEM((n, NA_HEADS // 2, 2 * GRID_W, 2 * NA_HEAD_DIM), jnp.float32)],
        compiler_params=_params("arbitrary"),
        name="neighbourhood_attention",
    )(qkv, qkv3, qkv3, *([qkv3] * (2 * n)), gates, *([bias_tab] * n))


HG_CHUNK = 64
HG_BLOCK = 256


SUBLANES = 8


def _hgrn_level_masks():
    c = HG_CHUNK
    t = np.arange(c)[:, None]
    s = np.arange(c)[None, :]
    out = []
    for reverse in (False, True):
        tt, ss = (c - 1 - t, c - 1 - s) if reverse else (t, s)
        x = tt ^ ss
        levels = [tt == ss]
        m = 1
        while m < c:
            levels.append((tt > ss) & (x >= m) & (x < 2 * m))
            m *= 2
        if len(levels) % 2:
            levels.append(np.zeros((c, c), bool))
        out.append(np.stack([np.concatenate(levels[i:i + 2], axis=1) for i in range(0, len(levels), 2)]))
    return np.stack(out).astype(np.float32)


def _hgrn_chunk(q, fl, v, lb, mask_ref, reverse):
    c, sub = HG_CHUNK, SUBLANES
    nv = c // sub
    bf16 = jnp.bfloat16
    en = jnp.exp2(jnp.abs(fl) * (-LOG2_E))
    big = 1.0 / (1.0 + en)
    small = en * big
    nonneg = fl >= 0.0
    f = lb + (1.0 - lb) * jnp.where(nonneg, big, small)
    g = jnp.log2(f)
    k = (1.0 - lb) * jnp.where(nonneg, small, big)

    srow = lax.broadcasted_iota(jnp.int32, (sub, HG_DIM), 0)
    cp = (sub - 1 - srow) if reverse else srow

    def prev_shift(x, j):
        return pltpu.roll(x, (sub - j) if reverse else j, 0)

    def row_bcast(x, p):
        i = (sub - 1 - p) if reverse else p
        return jnp.broadcast_to(x[i:i + 1, :], (sub, HG_DIM))

    def rows(lst):
        return jnp.concatenate(lst[::-1] if reverse else lst, axis=0)

    blocks = range(nv - 1, -1, -1) if reverse else range(nv)
    gs = [g[b * sub:(b + 1) * sub] for b in blocks]
    loc = []
    for x in gs:
        j = 1
        while j < sub:
            x = x + jnp.where(cp >= j, prev_shift(x, j), 0.0)
            j *= 2
        loc.append(x)
    carry = [None, row_bcast(loc[0], sub - 1)]
    a = [loc[0]]
    for i in range(1, nv):
        a.append(loc[i] + carry[i])
        carry.append(carry[i] + row_bcast(loc[i], sub - 1))
    total = carry[nv]

    def level_exponents(m):
        if m >= sub:
            w = m // sub
            out = []
            for i in range(nv):
                mid = carry[(i // (2 * w)) * (2 * w) + w]
                out.append(a[i] - mid if (i // w) % 2 else mid - a[i])
            return out
        if m == 1:
            return [jnp.where((cp & 1) == 1, x, 0.0) for x in gs]
        out = []
        for x in a:
            r = row_bcast(x, m - 1)
            for blk in range(1, sub // (2 * m)):
                r = jnp.where(cp < blk * 2 * m, r, row_bcast(x, blk * 2 * m + m - 1))
            out.append(-jnp.abs(x - r))
        return out

    d = 1 if reverse else 0
    qb, kb = q.astype(bf16), k.astype(bf16)
    groups = [(qb, kb)]
    m = 1
    while m < c:
        e = jnp.exp2(rows(level_exponents(m))).astype(bf16)
        groups.append((qb * e, kb * e))
        m *= 2

    zeros = jnp.zeros((c, HG_DIM), bf16)
    scores = None
    for slab in range(0, len(groups), 2):
        qa, ka = groups[slab]
        if slab + 1 < len(groups):
            qn, kn = groups[slab + 1]
            lhs = jnp.concatenate([qa, qn], axis=1)
            rhs = jnp.concatenate([jnp.concatenate([ka, zeros], axis=1),
                                   jnp.concatenate([zeros, kn], axis=1)], axis=0)
        else:
            lhs, rhs = qa, jnp.concatenate([ka, zeros], axis=0)
        part = mask_ref[d, slab // 2] * lax.dot_general(lhs, rhs, (((1,), (1,)), ((), ())),
                                                         preferred_element_type=jnp.float32)
        scores = part if scores is None else scores + part

    q_in = (q * jnp.exp2(rows(a))).astype(bf16)
    k_out = (k * jnp.exp2(rows([total - x for x in a]))).astype(bf16)
    vb = v.astype(bf16)
    return (scores.astype(bf16), q_in, k_out, jnp.exp2(total[0:1, :]),
            jnp.concatenate([vb, vb], axis=0), v.T.astype(bf16))


def _hgrn_chunk_state(intra, st):
    scores, q_in, k_out, decay, vv, vtb = intra
    o = jnp.dot(scores, vv, preferred_element_type=jnp.float32)
    o = o + lax.dot_general(q_in, st.astype(jnp.bfloat16), (((1,), (1,)), ((), ())),
                            preferred_element_type=jnp.float32)
    st_new = st * decay + jnp.dot(vtb, k_out, preferred_element_type=jnp.float32)
    return o, st_new


def _hgrn_kernel(lbl_ref, mask_ref, qf_ref, ff_ref, vf_ref, qb_ref, fb_ref, vb_ref,
                 of_ref, ob_ref, st_ref, *, layer):
    @pl.when(pl.program_id(0) == 0)
    def _():
        st_ref[...] = jnp.zeros_like(st_ref)

    lg = lbl_ref[...]
    mx = jnp.max(lg, axis=0, keepdims=True)
    ex = jnp.exp(lg - mx)
    lb = jnp.sum(ex[: layer + 1], axis=0) / jnp.sum(ex, axis=0)

    n = HG_BLOCK // HG_CHUNK
    refs = ((qf_ref, ff_ref, vf_ref, of_ref), (qb_ref, fb_ref, vb_ref, ob_ref))
    state = {}
    pending = None

    def finish(job):
        d, h, ci, rows, cols, intra = job
        st = st_ref[d, h] if ci == 0 else state[d, h]
        o, st = _hgrn_chunk_state(intra, st)
        refs[d][3][rows, cols] = o.astype(refs[d][3].dtype)
        if ci == n - 1:
            st_ref[d, h] = st
        else:
            state[d, h] = st

    for ci in range(n):
        for h in range(HG_HEADS):
            cols = slice(h * HG_DIM, (h + 1) * HG_DIM)
            for d in (0, 1):
                blk = ci if d == 0 else n - 1 - ci
                rows = slice(blk * HG_CHUNK, (blk + 1) * HG_CHUNK)
                q_ref, f_ref, v_ref, _ = refs[d]
                intra = _hgrn_chunk(q_ref[rows, cols], f_ref[rows, cols], v_ref[rows, cols],
                                    lb[d:d + 1, cols], mask_ref, d == 1)
                if pending is not None:
                    finish(pending)
                pending = (d, h, ci, rows, cols, intra)
    finish(pending)


def _hgrn2(rest, lb_logits, layer):
    t = rest.shape[0]
    nb = t // HG_BLOCK
    layers = lb_logits.shape[0]
    masks = jnp.asarray(_hgrn_level_masks())
    q_c, ff_c, fb_c, i_c = 1, 2, 3, 4
    blk = (HG_BLOCK, HG_WIDTH)
    fwd = lambda c: pl.BlockSpec(blk, lambda b: (b, c))
    bwd = lambda c: pl.BlockSpec(blk, lambda b: (nb - 1 - b, c))
    return pl.pallas_call(
        functools.partial(_hgrn_kernel, layer=layer),
        grid=(nb,),
        in_specs=[pl.BlockSpec((layers, 2, HG_WIDTH), lambda b: (0, 0, 0)),
                  pl.BlockSpec(masks.shape, lambda b: (0, 0, 0, 0)),
                  fwd(q_c), fwd(ff_c), fwd(i_c), bwd(q_c), bwd(fb_c), bwd(i_c)],
        out_specs=[pl.BlockSpec(blk, lambda b: (b, 0)),
                   pl.BlockSpec(blk, lambda b: (nb - 1 - b, 0))],
        out_shape=[jax.ShapeDtypeStruct((t, HG_WIDTH), jnp.bfloat16)] * 2,
        scratch_shapes=[pltpu.VMEM((2, HG_HEADS, HG_DIM, HG_DIM), jnp.float32)],
        compiler_params=_params("arbitrary"),
        name="hgrn2_bidirectional",
    )(lb_logits, masks, rest, rest, rest, rest, rest, rest)


def _merge_kernel(oa_ref, of_ref, ob_ref, zb_ref, g_ref, bg_ref, hgw_ref, w_ref, y_ref, wb16_ref):
    f32 = jnp.float32
    d = y_ref.shape[1]

    @pl.when(pl.program_id(0) == 0)
    def _():
        wb16_ref[...] = w_ref[...].astype(wb16_ref.dtype)

    wa_ref = wb16_ref.at[:NA_WIDTH]
    wb_ref = wb16_ref.at[NA_WIDTH:]
    zb = zb_ref[...].astype(f32)
    gate_b = zb * jax.nn.sigmoid(zb)
    osum = of_ref[...].astype(f32) + ob_ref[...].astype(f32)
    parts = []
    for h in range(HG_HEADS):
        sl = slice(h * HG_DIM, (h + 1) * HG_DIM)
        oh = osum[:, sl]
        ms = jnp.mean(oh * oh, axis=-1, keepdims=True)
        parts.append(oh * lax.rsqrt(ms + NORM_EPS) * hgw_ref[:, sl])
    o_b = (jnp.concatenate(parts, axis=-1) * gate_b).astype(jnp.bfloat16)

    pa = jnp.dot(oa_ref[...], wa_ref[...], preferred_element_type=f32)
    pb = jnp.dot(o_b, wb_ref[...], preferred_element_type=f32)
    y = (jax.nn.sigmoid(g_ref[:, :d].astype(f32) + bg_ref[:, :d]) * pa
         + jax.nn.sigmoid(g_ref[:, d:].astype(f32) + bg_ref[:, d:]) * pb)
    y_ref[...] = y.astype(y_ref.dtype)


def _out_proj_kernel(x_ref, y_ref, wo_ref, pw_ref, o_ref, wb_ref):
    @pl.when(pl.program_id(0) == 0)
    def _():
        wb_ref[...] = wo_ref[...].astype(wb_ref.dtype)

    u = jnp.dot(y_ref[...], wb_ref[...], preferred_element_type=jnp.float32)
    ms = jnp.mean(u * u, axis=-1, keepdims=True)
    o_ref[...] = x_ref[...] + u * lax.rsqrt(ms + NORM_EPS) * pw_ref[...]


def _const_spec(shape, r=0, c=0):
    return pl.BlockSpec(shape, lambda i: (r, c), pipeline_mode=pl.Buffered(1))


def _merge(oa, o_f, o_b, gz, b_gate, hg_norm_w, w_br, tm=512):
    t, d = oa.shape[0], w_br.shape[1]
    tile = lambda w, c: pl.BlockSpec((tm, w), lambda i: (i, c))
    return pl.pallas_call(
        _merge_kernel,
        grid=(t // tm,),
        in_specs=[tile(NA_WIDTH, 0), tile(HG_WIDTH, 0), tile(HG_WIDTH, 0),
                  tile(HG_WIDTH, 2 * d // HG_WIDTH), tile(2 * d, 0),
                  _const_spec((1, 2 * d)), _const_spec((1, HG_WIDTH)), _const_spec(w_br.shape)],
        out_specs=tile(d, 0),
        out_shape=jax.ShapeDtypeStruct((t, d), jnp.bfloat16),
        scratch_shapes=[pltpu.VMEM(w_br.shape, oa.dtype)],
        compiler_params=_params("arbitrary"),
        name="branch_merge",
    )(oa, o_f, o_b, gz, gz, b_gate.reshape(1, 2 * d), hg_norm_w.reshape(1, HG_WIDTH), w_br)


def _out_proj(x, y, w_o, post_w, tm=512):
    t, d = x.shape
    tile = pl.BlockSpec((tm, d), lambda i: (i, 0))
    return pl.pallas_call(
        _out_proj_kernel,
        grid=(t // tm,),
        in_specs=[tile, tile, _const_spec((d, d)), _const_spec((1, d))],
        out_specs=tile,
        out_shape=jax.ShapeDtypeStruct((t, d), jnp.float32),
        scratch_shapes=[pltpu.VMEM((d, d), y.dtype)],
        compiler_params=_params("arbitrary"),
        name="out_proj_norm",
    )(x, y, w_o, post_w.reshape(1, d))


def kernel(x, norm_pre_w, w_in, b_gate, na_rel_bias, hg_lb_logits, hg_norm_w, w_branch, w_out, norm_post_w):
    b, t, d = x.shape
    depth = w_in.shape[0]
    bf16 = jnp.bfloat16
    outs = []
    for bi in range(b):
        xb = x[bi]
        for l in range(depth):
            xn = _rmsnorm(xb, norm_pre_w[l])
            qkv = _in_proj(xn, w_in[l], 0, QKV_COLS, bf16, "in_proj_qkv",
                           first_block_scale=LOG2_E * NA_HEAD_DIM ** -0.5, tm=2048)
            rest = _in_proj(xn, w_in[l], QKV_COLS, NA_WIDTH + 4 * HG_WIDTH, jnp.float32, "in_proj_rest")
            gz = _in_proj(xn, w_in[l], QKV_COLS + NA_WIDTH + 4 * HG_WIDTH, HG_WIDTH + 2 * d, bf16,
                          "in_proj_gates", rotate=1, tm=2048)
            oa = _neighbourhood_attention(qkv, rest, _na_bias_table(na_rel_bias[l]))
            o_f, o_b = _hgrn2(rest, hg_lb_logits.astype(jnp.float32), l)
            y = _merge(oa, o_f, o_b, gz, b_gate[l], hg_norm_w[l], w_branch[l])
            xb = _out_proj(xb, y, w_out[l], norm_post_w[l])
        outs.append(xb)
    return outs[0][None] if b == 1 else jnp.stack(outs, axis=0)
```

```python
import functools

import jax
import jax.numpy as jnp
import numpy as np
from jax import lax
from jax.experimental import pallas as pl
from jax.experimental.pallas import tpu as pltpu

D_MODEL = 2048
GRID_W = 64
NA_HEAD_DIM = 64
NA_WIDTH = 1024
NA_HEADS = 16
NA_WIN_ROWS = 8
NA_WIN_COLS = 16
HG_DIM = 128
HG_HEADS = 8
HG_WIDTH = 1024
NORM_EPS = 1e-6
QKV_COLS = 3 * NA_WIDTH
MASK_VALUE = -1e30
LOG2_E = 1.4426950408889634

VMEM_LIMIT_BYTES = 56 * 1024 * 1024


def _params(*sem, vmem_limit_bytes=VMEM_LIMIT_BYTES):
    return pltpu.CompilerParams(dimension_semantics=sem, vmem_limit_bytes=vmem_limit_bytes)


def _rmsnorm_kernel(x_ref, w_ref, o_ref):
    x = x_ref[...]
    ms = jnp.mean(x * x, axis=-1, keepdims=True)
    o_ref[...] = (x * lax.rsqrt(ms + NORM_EPS) * w_ref[...]).astype(o_ref.dtype)


def _rmsnorm(x, w, tm=1024):
    t, d = x.shape
    return pl.pallas_call(
        _rmsnorm_kernel,
        grid=(t // tm,),
        in_specs=[pl.BlockSpec((tm, d), lambda i: (i, 0)),
                  pl.BlockSpec((1, d), lambda i: (0, 0))],
        out_specs=pl.BlockSpec((tm, d), lambda i: (i, 0)),
        out_shape=jax.ShapeDtypeStruct((t, d), jnp.bfloat16),
        compiler_params=_params("parallel"),
        name="rmsnorm_pre",
    )(x, w.reshape(1, d))


def _in_proj_kernel(a_ref, w_ref, o_ref, wb_ref, *, first_block_scale):
    @pl.when(pl.program_id(1) == 0)
    def _():
        wb_ref[...] = w_ref[...].astype(wb_ref.dtype)

    acc = jnp.dot(a_ref[...], wb_ref[...], preferred_element_type=jnp.float32)
    if first_block_scale is not None:
        acc = acc * jnp.where(pl.program_id(0) == 0, first_block_scale, 1.0)
    o_ref[...] = acc.astype(o_ref.dtype)


def _in_proj(a, w, col0, ncols, out_dtype, name, first_block_scale=None, rotate=0, tm=1024, tn=1024):
    m, k = a.shape
    assert col0 % tn == 0 and ncols % tn == 0 and m % tm == 0
    j0 = col0 // tn
    nblk = ncols // tn
    out_blk = lambda j: jnp.where(j < rotate, j + nblk - rotate, j - rotate)
    return pl.pallas_call(
        functools.partial(_in_proj_kernel, first_block_scale=first_block_scale),
        grid=(nblk, m // tm),
        in_specs=[pl.BlockSpec((tm, k), lambda j, i: (i, 0)),
                  pl.BlockSpec((k, tn), lambda j, i: (0, j0 + j))],
        out_specs=pl.BlockSpec((tm, tn), lambda j, i: (i, out_blk(j))),
        out_shape=jax.ShapeDtypeStruct((m, ncols), out_dtype),
        scratch_shapes=[pltpu.VMEM((k, tn), a.dtype)],
        compiler_params=_params("arbitrary", "arbitrary"),
        name=name,
    )(a, w)


LANES = 128


def _na_bias_kernel(rpb_ref, o_ref):
    var = pl.program_id(0)
    q = lax.broadcasted_iota(jnp.int32, (GRID_W, LANES), 0)
    lane = lax.broadcasted_iota(jnp.int32, (GRID_W, LANES), 1)
    k = lane & (GRID_W - 1)
    start = jnp.clip(q - NA_WIN_COLS // 2, 0, GRID_W - NA_WIN_COLS)
    inside = (k >= start) & (k < start + NA_WIN_COLS)
    left = lane < GRID_W

    def toeplitz(h, j, lane0):
        row = rpb_ref[h, pl.ds(NA_WIN_ROWS - 1 - var + j, 1), :]
        spread = jnp.broadcast_to(row, (GRID_W, LANES))
        shift = (lane0 - (NA_WIN_COLS - 1)) % LANES
        return pltpu.roll(spread, shift, 1, stride=1, stride_axis=0)

    for h in range(NA_HEADS):
        for j in range(0, NA_WIN_ROWS, 2):
            tile = jnp.where(left, toeplitz(h, j, 0), toeplitz(h, j + 1, GRID_W))
            o_ref[0, h, :, j * GRID_W:(j + 2) * GRID_W] = jnp.where(inside, tile * LOG2_E, MASK_VALUE)


def _na_bias_table(rpb):
    heads, nrow, ncol = rpb.shape
    band = NA_WIN_ROWS * GRID_W
    rpb_wide = jnp.pad(rpb.astype(jnp.float32), ((0, 0), (0, 0), (0, LANES - ncol)))
    return pl.pallas_call(
        _na_bias_kernel,
        grid=(NA_WIN_ROWS,),
        in_specs=[pl.BlockSpec((heads, nrow, LANES), lambda v: (0, 0, 0))],
        out_specs=pl.BlockSpec((1, heads, GRID_W, band), lambda v: (v, 0, 0, 0)),
        out_shape=jax.ShapeDtypeStruct((NA_WIN_ROWS, heads, GRID_W, band), jnp.float32),
        compiler_params=_params("parallel"),
        name="na_bias_table",
    )(rpb_wide)


NA_ROWS_PER_STEP = 8


def _na_kernel(q_ref, k0_ref, v0_ref, *refs, rows):
    n = NA_ROWS_PER_STEP
    kn_refs, vn_refs, z_ref = refs[:n], refs[n:2 * n], refs[2 * n]
    bias_refs = refs[2 * n + 1:3 * n + 1]
    o_ref, k_ring, v_ring, s_ref, m_ref = refs[3 * n + 1:]
    lanes = 2 * NA_HEAD_DIM
    band = NA_WIN_ROWS * GRID_W
    half = NA_WIN_ROWS // 2
    first = lax.broadcasted_iota(jnp.int32, (GRID_W, lanes), 1) < NA_HEAD_DIM
    pairs = NA_HEADS // 2
    ones = jnp.ones((band, lanes), jnp.bfloat16)

    step = pl.program_id(0)

    @pl.when(step == 0)
    def _():
        k_ring[...] = k0_ref[...]
        v_ring[...] = v0_ref[...]

    def row_state(i):
        r = step * NA_ROWS_PER_STEP + i
        enter = jnp.clip(r + half - 1, NA_WIN_ROWS - 1, rows - 1) & (NA_WIN_ROWS - 1)
        start = jnp.clip(r - half, 0, rows - NA_WIN_ROWS)
        return enter, [(start + j) & (NA_WIN_ROWS - 1) for j in range(NA_WIN_ROWS)]

    def score_phase(i, slots):
        qrows = slice(i * GRID_W, (i + 1) * GRID_W)
        for p in range(pairs):
            sl = slice(p * lanes, (p + 1) * lanes)
            qp = q_ref[qrows, sl]
            zero = jnp.zeros_like(qp)
            q_bd = jnp.concatenate([jnp.where(first, qp, zero), jnp.where(first, zero, qp)], axis=0)
            k_band = jnp.concatenate([k_ring[j, :, sl] for j in slots], axis=0)
            m = None
            for c0 in range(0, band, 2 * lanes):
                cs = slice(c0, c0 + 2 * lanes)
                s = lax.dot_general(q_bd, k_band[cs], (((1,), (1,)), ((), ())),
                                    preferred_element_type=jnp.float32) + bias_refs[i][0, p, :, cs]
                s_ref[i, p, :, cs] = s
                mh = jnp.max(s, axis=-1, keepdims=True)
                m = mh if m is None else jnp.maximum(m, mh)
            m_ref[i, p] = jnp.broadcast_to(m, (2 * GRID_W, lanes))

    def value_phase(i, slots):
        qrows = slice(i * GRID_W, (i + 1) * GRID_W)
        for p in range(pairs):
            sl = slice(p * lanes, (p + 1) * lanes)
            m = m_ref[i, p]
            e = jnp.exp2(s_ref[i, p] - jnp.concatenate([m] * (band // lanes), axis=-1))
            v_band = jnp.concatenate([v_ring[j, :, sl] for j in slots], axis=0)
            v_ext = jnp.concatenate([v_band, ones], axis=1)
            ol = jnp.dot(e.astype(jnp.bfloat16), v_ext, preferred_element_type=jnp.float32)
            o = ol[:, :lanes] / ol[:, lanes:]
            o = jnp.where(first, o[:GRID_W], o[GRID_W:])
            z = z_ref[qrows, sl]
            o_ref[qrows, sl] = (o * (z * jax.nn.sigmoid(z))).astype(o_ref.dtype)

    states = [row_state(i) for i in range(NA_ROWS_PER_STEP)]
    for i, (enter, slots) in enumerate(states):
        k_ring[enter] = kn_refs[i][0]
        score_phase(i, slots)
        if i > 0:
            value_phase(i - 1, states[i - 1][1])
        v_ring[enter] = vn_refs[i][0]
    value_phase(NA_ROWS_PER_STEP - 1, states[-1][1])


def _neighbourhood_attention(qkv, gates, bias_tab):
    t = qkv.shape[0]
    rows = t // GRID_W
    band = NA_WIN_ROWS * GRID_W

    def row_start(r):
        return jnp.clip(r - NA_WIN_ROWS // 2, 0, rows - NA_WIN_ROWS)

    n = NA_ROWS_PER_STEP
    assert rows % n == 0
    row_blk = (GRID_W, NA_WIDTH)
    step_blk = (n * GRID_W, NA_WIDTH)
    ring = (NA_WIN_ROWS, GRID_W, NA_WIDTH)
    half = NA_WIN_ROWS // 2
    first_rows = [pl.BlockSpec(ring, lambda s, c=c: (0, 0, c), pipeline_mode=pl.Buffered(1)) for c in (1, 2)]
    entering = [pl.BlockSpec((1,) + row_blk,
                             lambda s, c=c, i=i: (jnp.clip(s * n + i + half - 1, NA_WIN_ROWS - 1, rows - 1), 0, c))
                for c in (1, 2) for i in range(n)]
    bias = [pl.BlockSpec((1, NA_HEADS // 2, 2 * GRID_W, band),
                         lambda s, i=i: (s * n + i - row_start(s * n + i), 0, 0, 0)) for i in range(n)]
    qkv3 = qkv.reshape(rows, GRID_W, QKV_COLS)
    bias_tab = bias_tab.reshape(NA_WIN_ROWS, NA_HEADS // 2, 2 * GRID_W, band)
    return pl.pallas_call(
        functools.partial(_na_kernel, rows=rows),
        grid=(rows // n,),
        in_specs=([pl.BlockSpec(step_blk, lambda s: (s, 0))] + first_rows + entering
                  + [pl.BlockSpec(step_blk, lambda s: (s, 0))] + bias),
        out_specs=pl.BlockSpec(step_blk, lambda s: (s, 0)),
        out_shape=jax.ShapeDtypeStruct((t, NA_WIDTH), jnp.bfloat16),
        scratch_shapes=[pltpu.VMEM(ring, qkv.dtype), pltpu.VMEM(ring, qkv.dtype),
                        pltpu.VMEM((n, NA_HEADS // 2, 2 * GRID_W, band), jnp.float32),
                        pltpu.VMEM((n, NA_HEADS // 2, 2 * GRID_W, 2 * NA_HEAD_DIM), jnp.float32)],
        compiler_params=_params("arbitrary"),
        name="neighbourhood_attention",
    )(qkv, qkv3, qkv3, *([qkv3] * (2 * n)), gates, *([bias_tab] * n))


HG_CHUNK = 64
HG_BLOCK = 256


SUBLANES = 8


def _hgrn_level_masks():
    c = HG_CHUNK
    t = np.arange(c)[:, None]
    s = np.arange(c)[None, :]
    out = []
    for reverse in (False, True):
        tt, ss = (c - 1 - t, c - 1 - s) if reverse else (t, s)
        x = tt ^ ss
        levels = [tt == ss]
        m = 1
        while m < c:
            levels.append((tt > ss) & (x >= m) & (x < 2 * m))
            m *= 2
        if len(levels) % 2:
            levels.append(np.zeros((c, c), bool))
        out.append(np.stack([np.concatenate(levels[i:i + 2], axis=1) for i in range(0, len(levels), 2)]))
    return np.stack(out).astype(np.float32)


def _hgrn_chunk(q, fl, v, lb, mask_ref, reverse):
    c, sub = HG_CHUNK, SUBLANES
    nv = c // sub
    bf16 = jnp.bfloat16
    half_t = 0.5 * jnp.tanh(0.5 * fl)
    f = lb + (1.0 - lb) * (0.5 + half_t)
    g = jnp.log2(f)
    k = (1.0 - lb) * (0.5 - half_t)

    srow = lax.broadcasted_iota(jnp.int32, (sub, HG_DIM), 0)
    cp = (sub - 1 - srow) if reverse else srow

    def prev_shift(x, j):
        return pltpu.roll(x, (sub - j) if reverse else j, 0)

    def row_bcast(x, p):
        i = (sub - 1 - p) if reverse else p
        return jnp.broadcast_to(x[i:i + 1, :], (sub, HG_DIM))

    def rows(lst):
        return jnp.concatenate(lst[::-1] if reverse else lst, axis=0)

    blocks = range(nv - 1, -1, -1) if reverse else range(nv)
    gs = [g[b * sub:(b + 1) * sub] for b in blocks]
    loc = []
    for x in gs:
        j = 1
        while j < sub:
            x = x + jnp.where(cp >= j, prev_shift(x, j), 0.0)
            j *= 2
        loc.append(x)
    carry = [None, row_bcast(loc[0], sub - 1)]
    a = [loc[0]]
    for i in range(1, nv):
        a.append(loc[i] + carry[i])
        carry.append(carry[i] + row_bcast(loc[i], sub - 1))
    total = carry[nv]

    def level_exponents(m):
        if m >= sub:
            w = m // sub
            out = []
            for i in range(nv):
                mid = carry[(i // (2 * w)) * (2 * w) + w]
                out.append(a[i] - mid if (i // w) % 2 else mid - a[i])
            return out
        if m == 1:
            return [jnp.where((cp & 1) == 1, x, 0.0) for x in gs]
        out = []
        for x in a:
            r = row_bcast(x, m - 1)
            for blk in range(1, sub // (2 * m)):
                r = jnp.where(cp < blk * 2 * m, r, row_bcast(x, blk * 2 * m + m - 1))
            out.append(-jnp.abs(x - r))
        return out

    d = 1 if reverse else 0
    qb, kb = q.astype(bf16), k.astype(bf16)
    groups = [(qb, kb)]
    m = 1
    while m < c:
        e = jnp.exp2(rows(level_exponents(m))).astype(bf16)
        groups.append((qb * e, kb * e))
        m *= 2

    zeros = jnp.zeros((c, HG_DIM), bf16)
    scores = None
    for slab in range(0, len(groups), 2):
        qa, ka = groups[slab]
        if slab + 1 < len(groups):
            qn, kn = groups[slab + 1]
            lhs = jnp.concatenate([qa, qn], axis=1)
            rhs = jnp.concatenate([jnp.concatenate([ka, zeros], axis=1),
                                   jnp.concatenate([zeros, kn], axis=1)], axis=0)
        else:
            lhs, rhs = qa, jnp.concatenate([ka, zeros], axis=0)
        part = mask_ref[d, slab // 2] * lax.dot_general(lhs, rhs, (((1,), (1,)), ((), ())),
                                                         preferred_element_type=jnp.float32)
        scores = part if scores is None else scores + part

    q_in = (q * jnp.exp2(rows(a))).astype(bf16)
    k_out = (k * jnp.exp2(rows([total - x for x in a]))).astype(bf16)
    vb = v.astype(bf16)
    return (scores.astype(bf16), q_in, k_out, jnp.exp2(total[0:1, :]),
            jnp.concatenate([vb, vb], axis=0), v.T.astype(bf16))


def _hgrn_chunk_state(intra, st):
    scores, q_in, k_out, decay, vv, vtb = intra
    o = jnp.dot(scores, vv, preferred_element_type=jnp.float32)
    o = o + lax.dot_general(q_in, st.astype(jnp.bfloat16), (((1,), (1,)), ((), ())),
                            preferred_element_type=jnp.float32)
    st_new = st * decay + jnp.dot(vtb, k_out, preferred_element_type=jnp.float32)
    return o, st_new


def _hgrn_kernel(lbl_ref, mask_ref, qf_ref, ff_ref, vf_ref, qb_ref, fb_ref, vb_ref,
                 of_ref, ob_ref, st_ref, *, layer):
    @pl.when(pl.program_id(0) == 0)
    def _():
        st_ref[...] = jnp.zeros_like(st_ref)

    lg = lbl_ref[...]
    mx = jnp.max(lg, axis=0, keepdims=True)
    ex = jnp.exp(lg - mx)
    lb = jnp.sum(ex[: layer + 1], axis=0) / jnp.sum(ex, axis=0)

    n = HG_BLOCK // HG_CHUNK
    refs = ((qf_ref, ff_ref, vf_ref, of_ref), (qb_ref, fb_ref, vb_ref, ob_ref))
    state = {}
    pending = None

    def finish(job):
        d, h, ci, rows, cols, intra = job
        st = st_ref[d, h] if ci == 0 else state[d, h]
        o, st = _hgrn_chunk_state(intra, st)
        refs[d][3][rows, cols] = o.astype(refs[d][3].dtype)
        if ci == n - 1:
            st_ref[d, h] = st
        else:
            state[d, h] = st

    for ci in range(n):
        for h in range(HG_HEADS):
            cols = slice(h * HG_DIM, (h + 1) * HG_DIM)
            for d in (0, 1):
                blk = ci if d == 0 else n - 1 - ci
                rows = slice(blk * HG_CHUNK, (blk + 1) * HG_CHUNK)
                q_ref, f_ref, v_ref, _ = refs[d]
                intra = _hgrn_chunk(q_ref[rows, cols], f_ref[rows, cols], v_ref[rows, cols],
                                    lb[d:d + 1, cols], mask_ref, d == 1)
                if pending is not None:
                    finish(pending)
                pending = (d, h, ci, rows, cols, intra)
    finish(pending)


def _hgrn2(rest, lb_logits, layer):
    t = rest.shape[0]
    nb = t // HG_BLOCK
    layers = lb_logits.shape[0]
    masks = jnp.asarray(_hgrn_level_masks())
    q_c, ff_c, fb_c, i_c = 1, 2, 3, 4
    blk = (HG_BLOCK, HG_WIDTH)
    fwd = lambda c: pl.BlockSpec(blk, lambda b: (b, c))
    bwd = lambda c: pl.BlockSpec(blk, lambda b: (nb - 1 - b, c))
    return pl.pallas_call(
        functools.partial(_hgrn_kernel, layer=layer),
        grid=(nb,),
        in_specs=[pl.BlockSpec((layers, 2, HG_WIDTH), lambda b: (0, 0, 0)),
                  pl.BlockSpec(masks.shape, lambda b: (0, 0, 0, 0)),
                  fwd(q_c), fwd(ff_c), fwd(i_c), bwd(q_c), bwd(fb_c), bwd(i_c)],
        out_specs=[pl.BlockSpec(blk, lambda b: (b, 0)),
                   pl.BlockSpec(blk, lambda b: (nb - 1 - b, 0))],
        out_shape=[jax.ShapeDtypeStruct((t, HG_WIDTH), jnp.bfloat16)] * 2,
        scratch_shapes=[pltpu.VMEM((2, HG_HEADS, HG_DIM, HG_DIM), jnp.float32)],
        compiler_params=_params("arbitrary"),
        name="hgrn2_bidirectional",
    )(lb_logits, masks, rest, rest, rest, rest, rest, rest)


def _merge_kernel(oa_ref, of_ref, ob_ref, zb_ref, g_ref, bg_ref, hgw_ref, w_ref, y_ref, wb16_ref):
    f32 = jnp.float32
    d = y_ref.shape[1]

    @pl.when(pl.program_id(0) == 0)
    def _():
        wb16_ref[...] = w_ref[...].astype(wb16_ref.dtype)

    wa_ref = wb16_ref.at[:NA_WIDTH]
    wb_ref = wb16_ref.at[NA_WIDTH:]
    zb = zb_ref[...].astype(f32)
    gate_b = zb * jax.nn.sigmoid(zb)
    osum = of_ref[...].astype(f32) + ob_ref[...].astype(f32)
    parts = []
    for h in range(HG_HEADS):
        sl = slice(h * HG_DIM, (h + 1) * HG_DIM)
        oh = osum[:, sl]
        ms = jnp.mean(oh * oh, axis=-1, keepdims=True)
        parts.append(oh * lax.rsqrt(ms + NORM_EPS) * hgw_ref[:, sl])
    o_b = (jnp.concatenate(parts, axis=-1) * gate_b).astype(jnp.bfloat16)

    pa = jnp.dot(oa_ref[...], wa_ref[...], preferred_element_type=f32)
    pb = jnp.dot(o_b, wb_ref[...], preferred_element_type=f32)
    y = (jax.nn.sigmoid(g_ref[:, :d].astype(f32) + bg_ref[:, :d]) * pa
         + jax.nn.sigmoid(g_ref[:, d:].astype(f32) + bg_ref[:, d:]) * pb)
    y_ref[...] = y.astype(y_ref.dtype)


def _out_proj_kernel(x_ref, y_ref, wo_ref, pw_ref, o_ref, wb_ref):
    @pl.when(pl.program_id(0) == 0)
    def _():
        wb_ref[...] = wo_ref[...].astype(wb_ref.dtype)

    u = jnp.dot(y_ref[...], wb_ref[...], preferred_element_type=jnp.float32)
    ms = jnp.mean(u * u, axis=-1, keepdims=True)
    o_ref[...] = x_ref[...] + u * lax.rsqrt(ms + NORM_EPS) * pw_ref[...]


def _const_spec(shape, r=0, c=0):
    return pl.BlockSpec(shape, lambda i: (r, c), pipeline_mode=pl.Buffered(1))


def _merge(oa, o_f, o_b, gz, b_gate, hg_norm_w, w_br, tm=512):
    t, d = oa.shape[0], w_br.shape[1]
    tile = lambda w, c: pl.BlockSpec((tm, w), lambda i: (i, c))
    return pl.pallas_call(
        _merge_kernel,
        grid=(t // tm,),
        in_specs=[tile(NA_WIDTH, 0), tile(HG_WIDTH, 0), tile(HG_WIDTH, 0),
                  tile(HG_WIDTH, 2 * d // HG_WIDTH), tile(2 * d, 0),
                  _const_spec((1, 2 * d)), _const_spec((1, HG_WIDTH)), _const_spec(w_br.shape)],
        out_specs=tile(d, 0),
        out_shape=jax.ShapeDtypeStruct((t, d), jnp.bfloat16),
        scratch_shapes=[pltpu.VMEM(w_br.shape, oa.dtype)],
        compiler_params=_params("arbitrary"),
        name="branch_merge",
    )(oa, o_f, o_b, gz, gz, b_gate.reshape(1, 2 * d), hg_norm_w.reshape(1, HG_WIDTH), w_br)


def _out_proj(x, y, w_o, post_w, tm=512):
    t, d = x.shape
    tile = pl.BlockSpec((tm, d), lambda i: (i, 0))
    return pl.pallas_call(
        _out_proj_kernel,
        grid=(t // tm,),
        in_specs=[tile, tile, _const_spec((d, d)), _const_spec((1, d))],
        out_specs=tile,
        out_shape=jax.ShapeDtypeStruct((t, d), jnp.float32),
        scratch_shapes=[pltpu.VMEM((d, d), y.dtype)],
        compiler_params=_params("arbitrary"),
        name="out_proj_norm",
    )(x, y, w_o, post_w.reshape(1, d))


def kernel(x, norm_pre_w, w_in, b_gate, na_rel_bias, hg_lb_logits, hg_norm_w, w_branch, w_out, norm_post_w):
    b, t, d = x.shape
    depth = w_in.shape[0]
    bf16 = jnp.bfloat16
    outs = []
    for bi in range(b):
        xb = x[bi]
        for l in range(depth):
            xn = _rmsnorm(xb, norm_pre_w[l])
            qkv = _in_proj(xn, w_in[l], 0, QKV_COLS, bf16, "in_proj_qkv",
                           first_block_scale=LOG2_E * NA_HEAD_DIM ** -0.5, tm=2048)
            rest = _in_proj(xn, w_in[l], QKV_COLS, NA_WIDTH + 4 * HG_WIDTH, jnp.float32, "in_proj_rest")
            gz = _in_proj(xn, w_in[l], QKV_COLS + NA_WIDTH + 4 * HG_WIDTH, HG_WIDTH + 2 * d, bf16,
                          "in_proj_gates", rotate=1, tm=2048)
            oa = _neighbourhood_attention(qkv, rest, _na_bias_table(na_rel_bias[l]))
            o_f, o_b = _hgrn2(rest, hg_lb_logits.astype(jnp.float32), l)
            y = _merge(oa, o_f, o_b, gz, b_gate[l], hg_norm_w[l], w_branch[l])
            xb = _out_proj(xb, y, w_out[l], norm_post_w[l])
        outs.append(xb)
    return outs[0][None] if b == 1 else jnp.stack(outs, axis=0)
```

```python
import functools

import jax
import jax.numpy as jnp
import numpy as np
from jax import lax
from jax.experimental import pallas as pl
from jax.experimental.pallas import tpu as pltpu

D_MODEL = 2048
GRID_W = 64
NA_HEAD_DIM = 64
NA_WIDTH = 1024
NA_HEADS = 16
NA_WIN_ROWS = 8
NA_WIN_COLS = 16
HG_DIM = 128
HG_HEADS = 8
HG_WIDTH = 1024
NORM_EPS = 1e-6
QKV_COLS = 3 * NA_WIDTH
MASK_VALUE = -1e30
LOG2_E = 1.4426950408889634

VMEM_LIMIT_BYTES = 56 * 1024 * 1024


def _params(*sem, vmem_limit_bytes=VMEM_LIMIT_BYTES):
    return pltpu.CompilerParams(dimension_semantics=sem, vmem_limit_bytes=vmem_limit_bytes)


def _sigmoid(z):
    return 0.5 * jnp.tanh(0.5 * z) + 0.5


def _rmsnorm_kernel(x_ref, w_ref, o_ref):
    x = x_ref[...]
    ms = jnp.mean(x * x, axis=-1, keepdims=True)
    o_ref[...] = (x * lax.rsqrt(ms + NORM_EPS) * w_ref[...]).astype(o_ref.dtype)


def _rmsnorm(x, w, tm=1024):
    t, d = x.shape
    return pl.pallas_call(
        _rmsnorm_kernel,
        grid=(t // tm,),
        in_specs=[pl.BlockSpec((tm, d), lambda i: (i, 0)),
                  pl.BlockSpec((1, d), lambda i: (0, 0))],
        out_specs=pl.BlockSpec((tm, d), lambda i: (i, 0)),
        out_shape=jax.ShapeDtypeStruct((t, d), jnp.bfloat16),
        compiler_params=_params("parallel"),
        name="rmsnorm_pre",
    )(x, w.reshape(1, d))


def _in_proj_kernel(a_ref, w_ref, o_ref, wb_ref, *, first_block_scale):
    @pl.when(pl.program_id(1) == 0)
    def _():
        wb_ref[...] = w_ref[...].astype(wb_ref.dtype)

    acc = jnp.dot(a_ref[...], wb_ref[...], preferred_element_type=jnp.float32)
    if first_block_scale is not None:
        acc = acc * jnp.where(pl.program_id(0) == 0, first_block_scale, 1.0)
    o_ref[...] = acc.astype(o_ref.dtype)


def _in_proj(a, w, col0, ncols, out_dtype, name, first_block_scale=None, rotate=0, tm=1024, tn=1024):
    m, k = a.shape
    assert col0 % tn == 0 and ncols % tn == 0 and m % tm == 0
    j0 = col0 // tn
    nblk = ncols // tn
    out_blk = lambda j: jnp.where(j < rotate, j + nblk - rotate, j - rotate)
    return pl.pallas_call(
        functools.partial(_in_proj_kernel, first_block_scale=first_block_scale),
        grid=(nblk, m // tm),
        in_specs=[pl.BlockSpec((tm, k), lambda j, i: (i, 0)),
                  pl.BlockSpec((k, tn), lambda j, i: (0, j0 + j))],
        out_specs=pl.BlockSpec((tm, tn), lambda j, i: (i, out_blk(j))),
        out_shape=jax.ShapeDtypeStruct((m, ncols), out_dtype),
        scratch_shapes=[pltpu.VMEM((k, tn), a.dtype)],
        compiler_params=_params("arbitrary", "arbitrary"),
        name=name,
    )(a, w)


LANES = 128


def _na_bias_kernel(rpb_ref, o_ref):
    var = pl.program_id(0)
    q = lax.broadcasted_iota(jnp.int32, (GRID_W, LANES), 0)
    lane = lax.broadcasted_iota(jnp.int32, (GRID_W, LANES), 1)
    k = lane & (GRID_W - 1)
    start = jnp.clip(q - NA_WIN_COLS // 2, 0, GRID_W - NA_WIN_COLS)
    inside = (k >= start) & (k < start + NA_WIN_COLS)
    left = lane < GRID_W

    def toeplitz(h, j, lane0):
        row = rpb_ref[h, pl.ds(NA_WIN_ROWS - 1 - var + j, 1), :]
        spread = jnp.broadcast_to(row, (GRID_W, LANES))
        shift = (lane0 - (NA_WIN_COLS - 1)) % LANES
        return pltpu.roll(spread, shift, 1, stride=1, stride_axis=0)

    for h in range(NA_HEADS):
        for j in range(0, NA_WIN_ROWS, 2):
            tile = jnp.where(left, toeplitz(h, j, 0), toeplitz(h, j + 1, GRID_W))
            o_ref[0, h, :, j * GRID_W:(j + 2) * GRID_W] = jnp.where(inside, tile * LOG2_E, MASK_VALUE)


def _na_bias_table(rpb):
    heads, nrow, ncol = rpb.shape
    band = NA_WIN_ROWS * GRID_W
    rpb_wide = jnp.pad(rpb.astype(jnp.float32), ((0, 0), (0, 0), (0, LANES - ncol)))
    return pl.pallas_call(
        _na_bias_kernel,
        grid=(NA_WIN_ROWS,),
        in_specs=[pl.BlockSpec((heads, nrow, LANES), lambda v: (0, 0, 0))],
        out_specs=pl.BlockSpec((1, heads, GRID_W, band), lambda v: (v, 0, 0, 0)),
        out_shape=jax.ShapeDtypeStruct((NA_WIN_ROWS, heads, GRID_W, band), jnp.float32),
        compiler_params=_params("parallel"),
        name="na_bias_table",
    )(rpb_wide)


NA_ROWS_PER_STEP = 8


def _na_kernel(q_ref, k0_ref, v0_ref, *refs, rows):
    n = NA_ROWS_PER_STEP
    kn_refs, vn_refs, z_ref = refs[:n], refs[n:2 * n], refs[2 * n]
    bias_refs = refs[2 * n + 1:3 * n + 1]
    o_ref, k_ring, v_ring, s_ref, m_ref = refs[3 * n + 1:]
    lanes = 2 * NA_HEAD_DIM
    band = NA_WIN_ROWS * GRID_W
    half = NA_WIN_ROWS // 2
    first = lax.broadcasted_iota(jnp.int32, (GRID_W, lanes), 1) < NA_HEAD_DIM
    pairs = NA_HEADS // 2
    ones = jnp.ones((band, lanes), jnp.bfloat16)

    step = pl.program_id(0)

    @pl.when(step == 0)
    def _():
        k_ring[...] = k0_ref[...]
        v_ring[...] = v0_ref[...]

    def row_state(i):
        r = step * NA_ROWS_PER_STEP + i
        enter = jnp.clip(r + half - 1, NA_WIN_ROWS - 1, rows - 1) & (NA_WIN_ROWS - 1)
        start = jnp.clip(r - half, 0, rows - NA_WIN_ROWS)
        return enter, [(start + j) & (NA_WIN_ROWS - 1) for j in range(NA_WIN_ROWS)]

    def score_phase(i, slots):
        qrows = slice(i * GRID_W, (i + 1) * GRID_W)
        for p in range(pairs):
            sl = slice(p * lanes, (p + 1) * lanes)
            qp = q_ref[qrows, sl]
            zero = jnp.zeros_like(qp)
            q_bd = jnp.concatenate([jnp.where(first, qp, zero), jnp.where(first, zero, qp)], axis=0)
            k_band = jnp.concatenate([k_ring[j, :, sl] for j in slots], axis=0)
            m = None
            for c0 in range(0, band, 2 * lanes):
                cs = slice(c0, c0 + 2 * lanes)
                s = lax.dot_general(q_bd, k_band[cs], (((1,), (1,)), ((), ())),
                                    preferred_element_type=jnp.float32) + bias_refs[i][0, p, :, cs]
                s_ref[i, p, :, cs] = s
                mh = jnp.max(s, axis=-1, keepdims=True)
                m = mh if m is None else jnp.maximum(m, mh)
            m_ref[i, p] = jnp.broadcast_to(m, (2 * GRID_W, lanes))

    def value_phase(i, slots):
        qrows = slice(i * GRID_W, (i + 1) * GRID_W)
        for p in range(pairs):
            sl = slice(p * lanes, (p + 1) * lanes)
            m = m_ref[i, p]
            e = jnp.exp2(s_ref[i, p] - jnp.concatenate([m] * (band // lanes), axis=-1))
            v_band = jnp.concatenate([v_ring[j, :, sl] for j in slots], axis=0)
            v_ext = jnp.concatenate([v_band, ones], axis=1)
            ol = jnp.dot(e.astype(jnp.bfloat16), v_ext, preferred_element_type=jnp.float32)
            o = ol[:, :lanes] / ol[:, lanes:]
            o = jnp.where(first, o[:GRID_W], o[GRID_W:])
            z = z_ref[qrows, sl]
            o_ref[qrows, sl] = (o * (z * _sigmoid(z))).astype(o_ref.dtype)

    states = [row_state(i) for i in range(NA_ROWS_PER_STEP)]
    for i, (enter, slots) in enumerate(states):
        k_ring[enter] = kn_refs[i][0]
        score_phase(i, slots)
        if i > 0:
            value_phase(i - 1, states[i - 1][1])
        v_ring[enter] = vn_refs[i][0]
    value_phase(NA_ROWS_PER_STEP - 1, states[-1][1])


def _neighbourhood_attention(qkv, gates, bias_tab):
    t = qkv.shape[0]
    rows = t // GRID_W
    band = NA_WIN_ROWS * GRID_W

    def row_start(r):
        return jnp.clip(r - NA_WIN_ROWS // 2, 0, rows - NA_WIN_ROWS)

    n = NA_ROWS_PER_STEP
    assert rows % n == 0
    row_blk = (GRID_W, NA_WIDTH)
    step_blk = (n * GRID_W, NA_WIDTH)
    ring = (NA_WIN_ROWS, GRID_W, NA_WIDTH)
    half = NA_WIN_ROWS // 2
    first_rows = [pl.BlockSpec(ring, lambda s, c=c: (0, 0, c), pipeline_mode=pl.Buffered(1)) for c in (1, 2)]
    entering = [pl.BlockSpec((1,) + row_blk,
                             lambda s, c=c, i=i: (jnp.clip(s * n + i + half - 1, NA_WIN_ROWS - 1, rows - 1), 0, c))
                for c in (1, 2) for i in range(n)]
    bias = [pl.BlockSpec((1, NA_HEADS // 2, 2 * GRID_W, band),
                         lambda s, i=i: (s * n + i - row_start(s * n + i), 0, 0, 0)) for i in range(n)]
    qkv3 = qkv.reshape(rows, GRID_W, QKV_COLS)
    bias_tab = bias_tab.reshape(NA_WIN_ROWS, NA_HEADS // 2, 2 * GRID_W, band)
    return pl.pallas_call(
        functools.partial(_na_kernel, rows=rows),
        grid=(rows // n,),
        in_specs=([pl.BlockSpec(step_blk, lambda s: (s, 0))] + first_rows + entering
                  + [pl.BlockSpec(step_blk, lambda s: (s, 0))] + bias),
        out_specs=pl.BlockSpec(step_blk, lambda s: (s, 0)),
        out_shape=jax.ShapeDtypeStruct((t, NA_WIDTH), jnp.bfloat16),
        scratch_shapes=[pltpu.VMEM(ring, qkv.dtype), pltpu.VMEM(ring, qkv.dtype),
                        pltpu.VMEM((n, NA_HEADS // 2, 2 * GRID_W, band), jnp.float32),
                        pltpu.VMEM((n, NA_HEADS // 2, 2 * GRID_W, 2 * NA_HEAD_DIM), jnp.float32)],
        compiler_params=_params("arbitrary"),
        name="neighbourhood_attention",
    )(qkv, qkv3, qkv3, *([qkv3] * (2 * n)), gates, *([bias_tab] * n))


HG_CHUNK = 64
HG_BLOCK = 256


SUBLANES = 8


def _hgrn_level_masks():
    c = HG_CHUNK
    t = np.arange(c)[:, None]
    s = np.arange(c)[None, :]
    out = []
    for reverse in (False, True):
        tt, ss = (c - 1 - t, c - 1 - s) if reverse else (t, s)
        x = tt ^ ss
        levels = [tt == ss]
        m = 1
        while m < c:
            levels.append((tt > ss) & (x >= m) & (x < 2 * m))
            m *= 2
        if len(levels) % 2:
            levels.append(np.zeros((c, c), bool))
        out.append(np.stack([np.concatenate(levels[i:i + 2], axis=1) for i in range(0, len(levels), 2)]))
    return np.stack(out).astype(np.float32)


def _hgrn_chunk(q, fl, v, lb, mask_ref, reverse):
    c, sub = HG_CHUNK, SUBLANES
    nv = c // sub
    bf16 = jnp.bfloat16
    half_t = 0.5 * jnp.tanh(0.5 * fl)
    f = lb + (1.0 - lb) * (0.5 + half_t)
    g = jnp.log2(f)
    k = (1.0 - lb) * (0.5 - half_t)

    srow = lax.broadcasted_iota(jnp.int32, (sub, HG_DIM), 0)
    cp = (sub - 1 - srow) if reverse else srow

    def prev_shift(x, j):
        return pltpu.roll(x, (sub - j) if reverse else j, 0)

    def row_bcast(x, p):
        i = (sub - 1 - p) if reverse else p
        return jnp.broadcast_to(x[i:i + 1, :], (sub, HG_DIM))

    def rows(lst):
        return jnp.concatenate(lst[::-1] if reverse else lst, axis=0)

    blocks = range(nv - 1, -1, -1) if reverse else range(nv)
    gs = [g[b * sub:(b + 1) * sub] for b in blocks]
    loc = []
    for x in gs:
        j = 1
        while j < sub:
            x = x + jnp.where(cp >= j, prev_shift(x, j), 0.0)
            j *= 2
        loc.append(x)
    carry = [None, row_bcast(loc[0], sub - 1)]
    a = [loc[0]]
    for i in range(1, nv):
        a.append(loc[i] + carry[i])
        carry.append(carry[i] + row_bcast(loc[i], sub - 1))
    total = carry[nv]

    def level_exponents(m):
        if m >= sub:
            w = m // sub
            out = []
            for i in range(nv):
                mid = carry[(i // (2 * w)) * (2 * w) + w]
                out.append(a[i] - mid if (i // w) % 2 else mid - a[i])
            return out
        if m == 1:
            return [jnp.where((cp & 1) == 1, x, 0.0) for x in gs]
        out = []
        for x in a:
            r = row_bcast(x, m - 1)
            for blk in range(1, sub // (2 * m)):
                r = jnp.where(cp < blk * 2 * m, r, row_bcast(x, blk * 2 * m + m - 1))
            out.append(-jnp.abs(x - r))
        return out

    d = 1 if reverse else 0
    qb, kb = q.astype(bf16), k.astype(bf16)
    groups = [(qb, kb)]
    m = 1
    while m < c:
        e = jnp.exp2(rows(level_exponents(m))).astype(bf16)
        groups.append((qb * e, kb * e))
        m *= 2

    zeros = jnp.zeros((c, HG_DIM), bf16)
    scores = None
    for slab in range(0, len(groups), 2):
        qa, ka = groups[slab]
        if slab + 1 < len(groups):
            qn, kn = groups[slab + 1]
            lhs = jnp.concatenate([qa, qn], axis=1)
            rhs = jnp.concatenate([jnp.concatenate([ka, zeros], axis=1),
                                   jnp.concatenate([zeros, kn], axis=1)], axis=0)
        else:
            lhs, rhs = qa, jnp.concatenate([ka, zeros], axis=0)
        part = mask_ref[d, slab // 2] * lax.dot_general(lhs, rhs, (((1,), (1,)), ((), ())),
                                                         preferred_element_type=jnp.float32)
        scores = part if scores is None else scores + part

    q_in = (q * jnp.exp2(rows(a))).astype(bf16)
    k_out = (k * jnp.exp2(rows([total - x for x in a]))).astype(bf16)
    vb = v.astype(bf16)
    return (scores.astype(bf16), q_in, k_out, jnp.exp2(total[0:1, :]),
            jnp.concatenate([vb, vb], axis=0), v.T.astype(bf16))


def _hgrn_chunk_state(intra, st):
    scores, q_in, k_out, decay, vv, vtb = intra
    o = jnp.dot(scores, vv, preferred_element_type=jnp.float32)
    o = o + lax.dot_general(q_in, st.astype(jnp.bfloat16), (((1,), (1,)), ((), ())),
                            preferred_element_type=jnp.float32)
    st_new = st * decay + jnp.dot(vtb, k_out, preferred_element_type=jnp.float32)
    return o, st_new


def _hgrn_kernel(lbl_ref, mask_ref, qf_ref, ff_ref, vf_ref, qb_ref, fb_ref, vb_ref,
                 of_ref, ob_ref, st_ref, *, layer):
    @pl.when(pl.program_id(0) == 0)
    def _():
        st_ref[...] = jnp.zeros_like(st_ref)

    lg = lbl_ref[...]
    mx = jnp.max(lg, axis=0, keepdims=True)
    ex = jnp.exp(lg - mx)
    lb = jnp.sum(ex[: layer + 1], axis=0) / jnp.sum(ex, axis=0)

    n = HG_BLOCK // HG_CHUNK
    refs = ((qf_ref, ff_ref, vf_ref, of_ref), (qb_ref, fb_ref, vb_ref, ob_ref))
    state = {}
    pending = None

    def finish(job):
        d, h, ci, rows, cols, intra = job
        st = st_ref[d, h] if ci == 0 else state[d, h]
        o, st = _hgrn_chunk_state(intra, st)
        refs[d][3][rows, cols] = o.astype(refs[d][3].dtype)
        if ci == n - 1:
            st_ref[d, h] = st
        else:
            state[d, h] = st

    for ci in range(n):
        for h in range(HG_HEADS):
            cols = slice(h * HG_DIM, (h + 1) * HG_DIM)
            for d in (0, 1):
                blk = ci if d == 0 else n - 1 - ci
                rows = slice(blk * HG_CHUNK, (blk + 1) * HG_CHUNK)
                q_ref, f_ref, v_ref, _ = refs[d]
                intra = _hgrn_chunk(q_ref[rows, cols], f_ref[rows, cols], v_ref[rows, cols],
                                    lb[d:d + 1, cols], mask_ref, d == 1)
                if pending is not None:
                    finish(pending)
                pending = (d, h, ci, rows, cols, intra)
    finish(pending)


def _hgrn2(rest, lb_logits, layer):
    t = rest.shape[0]
    nb = t // HG_BLOCK
    layers = lb_logits.shape[0]
    masks = jnp.asarray(_hgrn_level_masks())
    q_c, ff_c, fb_c, i_c = 1, 2, 3, 4
    blk = (HG_BLOCK, HG_WIDTH)
    fwd = lambda c: pl.BlockSpec(blk, lambda b: (b, c))
    bwd = lambda c: pl.BlockSpec(blk, lambda b: (nb - 1 - b, c))
    return pl.pallas_call(
        functools.partial(_hgrn_kernel, layer=layer),
        grid=(nb,),
        in_specs=[pl.BlockSpec((layers, 2, HG_WIDTH), lambda b: (0, 0, 0)),
                  pl.BlockSpec(masks.shape, lambda b: (0, 0, 0, 0)),
                  fwd(q_c), fwd(ff_c), fwd(i_c), bwd(q_c), bwd(fb_c), bwd(i_c)],
        out_specs=[pl.BlockSpec(blk, lambda b: (b, 0)),
                   pl.BlockSpec(blk, lambda b: (nb - 1 - b, 0))],
        out_shape=[jax.ShapeDtypeStruct((t, HG_WIDTH), jnp.bfloat16)] * 2,
        scratch_shapes=[pltpu.VMEM((2, HG_HEADS, HG_DIM, HG_DIM), jnp.float32)],
        compiler_params=_params("arbitrary"),
        name="hgrn2_bidirectional",
    )(lb_logits, masks, rest, rest, rest, rest, rest, rest)


def _merge_kernel(oa_ref, of_ref, ob_ref, zb_ref, g_ref, bg_ref, hgw_ref, w_ref, y_ref, wb16_ref):
    f32 = jnp.float32
    d = y_ref.shape[1]

    @pl.when(pl.program_id(0) == 0)
    def _():
        wb16_ref[...] = w_ref[...].astype(wb16_ref.dtype)

    wa_ref = wb16_ref.at[:NA_WIDTH]
    wb_ref = wb16_ref.at[NA_WIDTH:]
    zb = zb_ref[...].astype(f32)
    gate_b = zb * _sigmoid(zb)
    osum = of_ref[...].astype(f32) + ob_ref[...].astype(f32)
    parts = []
    for h in range(HG_HEADS):
        sl = slice(h * HG_DIM, (h + 1) * HG_DIM)
        oh = osum[:, sl]
        ms = jnp.mean(oh * oh, axis=-1, keepdims=True)
        parts.append(oh * lax.rsqrt(ms + NORM_EPS) * hgw_ref[:, sl])
    o_b = (jnp.concatenate(parts, axis=-1) * gate_b).astype(jnp.bfloat16)

    pa = jnp.dot(oa_ref[...], wa_ref[...], preferred_element_type=f32)
    pb = jnp.dot(o_b, wb_ref[...], preferred_element_type=f32)
    y = (_sigmoid(g_ref[:, :d].astype(f32) + bg_ref[:, :d]) * pa
         + _sigmoid(g_ref[:, d:].astype(f32) + bg_ref[:, d:]) * pb)
    y_ref[...] = y.astype(y_ref.dtype)


def _out_proj_kernel(x_ref, y_ref, wo_ref, pw_ref, o_ref, wb_ref):
    @pl.when(pl.program_id(0) == 0)
    def _():
        wb_ref[...] = wo_ref[...].astype(wb_ref.dtype)

    u = jnp.dot(y_ref[...], wb_ref[...], preferred_element_type=jnp.float32)
    ms = jnp.mean(u * u, axis=-1, keepdims=True)
    o_ref[...] = x_ref[...] + u * lax.rsqrt(ms + NORM_EPS) * pw_ref[...]


def _const_spec(shape, r=0, c=0):
    return pl.BlockSpec(shape, lambda i: (r, c), pipeline_mode=pl.Buffered(1))


def _merge(oa, o_f, o_b, gz, b_gate, hg_norm_w, w_br, tm=512):
    t, d = oa.shape[0], w_br.shape[1]
    tile = lambda w, c: pl.BlockSpec((tm, w), lambda i: (i, c))
    return pl.pallas_call(
        _merge_kernel,
        grid=(t // tm,),
        in_specs=[tile(NA_WIDTH, 0), tile(HG_WIDTH, 0), tile(HG_WIDTH, 0),
                  tile(HG_WIDTH, 2 * d // HG_WIDTH), tile(2 * d, 0),
                  _const_spec((1, 2 * d)), _const_spec((1, HG_WIDTH)), _const_spec(w_br.shape)],
        out_specs=tile(d, 0),
        out_shape=jax.ShapeDtypeStruct((t, d), jnp.bfloat16),
        scratch_shapes=[pltpu.VMEM(w_br.shape, oa.dtype)],
        compiler_params=_params("arbitrary"),
        name="branch_merge",
    )(oa, o_f, o_b, gz, gz, b_gate.reshape(1, 2 * d), hg_norm_w.reshape(1, HG_WIDTH), w_br)


def _out_proj(x, y, w_o, post_w, tm=512):
    t, d = x.shape
    tile = pl.BlockSpec((tm, d), lambda i: (i, 0))
    return pl.pallas_call(
        _out_proj_kernel,
        grid=(t // tm,),
        in_specs=[tile, tile, _const_spec((d, d)), _const_spec((1, d))],
        out_specs=tile,
        out_shape=jax.ShapeDtypeStruct((t, d), jnp.float32),
        scratch_shapes=[pltpu.VMEM((d, d), y.dtype)],
        compiler_params=_params("arbitrary"),
        name="out_proj_norm",
    )(x, y, w_o, post_w.reshape(1, d))


def kernel(x, norm_pre_w, w_in, b_gate, na_rel_bias, hg_lb_logits, hg_norm_w, w_branch, w_out, norm_post_w):
    b, t, d = x.shape
    depth = w_in.shape[0]
    bf16 = jnp.bfloat16
    outs = []
    for bi in range(b):
        xb = x[bi]
        for l in range(depth):
            xn = _rmsnorm(xb, norm_pre_w[l])
            qkv = _in_proj(xn, w_in[l], 0, QKV_COLS, bf16, "in_proj_qkv",
                           first_block_scale=LOG2_E * NA_HEAD_DIM ** -0.5, tm=2048)
            rest = _in_proj(xn, w_in[l], QKV_COLS, NA_WIDTH + 4 * HG_WIDTH, jnp.float32, "in_proj_rest")
            gz = _in_proj(xn, w_in[l], QKV_COLS + NA_WIDTH + 4 * HG_WIDTH, HG_WIDTH + 2 * d, bf16,
                          "in_proj_gates", rotate=1, tm=2048)
            oa = _neighbourhood_attention(qkv, rest, _na_bias_table(na_rel_bias[l]))
            o_f, o_b = _hgrn2(rest, hg_lb_logits.astype(jnp.float32), l)
            y = _merge(oa, o_f, o_b, gz, b_gate[l], hg_norm_w[l], w_branch[l])
            xb = _out_proj(xb, y, w_out[l], norm_post_w[l])
        outs.append(xb)
    return outs[0][None] if b == 1 else jnp.stack(outs, axis=0)
```

```python
import functools

import jax
import jax.numpy as jnp
import numpy as np
from jax import lax
from jax.experimental import pallas as pl
from jax.experimental.pallas import tpu as pltpu

D_MODEL = 2048
GRID_W = 64
NA_HEAD_DIM = 64
NA_WIDTH = 1024
NA_HEADS = 16
NA_WIN_ROWS = 8
NA_WIN_COLS = 16
HG_DIM = 128
HG_HEADS = 8
HG_WIDTH = 1024
NORM_EPS = 1e-6
QKV_COLS = 3 * NA_WIDTH
MASK_VALUE = -1e30
LOG2_E = 1.4426950408889634

VMEM_LIMIT_BYTES = 56 * 1024 * 1024


def _params(*sem, vmem_limit_bytes=VMEM_LIMIT_BYTES):
    return pltpu.CompilerParams(dimension_semantics=sem, vmem_limit_bytes=vmem_limit_bytes)


def _rmsnorm_kernel(x_ref, w_ref, o_ref):
    x = x_ref[...]
    ms = jnp.mean(x * x, axis=-1, keepdims=True)
    o_ref[...] = (x * lax.rsqrt(ms + NORM_EPS) * w_ref[...]).astype(o_ref.dtype)


def _rmsnorm(x, w, tm=1024):
    t, d = x.shape
    return pl.pallas_call(
        _rmsnorm_kernel,
        grid=(t // tm,),
        in_specs=[pl.BlockSpec((tm, d), lambda i: (i, 0)),
                  pl.BlockSpec((1, d), lambda i: (0, 0))],
        out_specs=pl.BlockSpec((tm, d), lambda i: (i, 0)),
        out_shape=jax.ShapeDtypeStruct((t, d), jnp.bfloat16),
        compiler_params=_params("parallel"),
        name="rmsnorm_pre",
    )(x, w.reshape(1, d))


def _in_proj_kernel(a_ref, w_ref, o_ref, wb_ref, *, first_block_scale):
    @pl.when(pl.program_id(1) == 0)
    def _():
        wb_ref[...] = w_ref[...].astype(wb_ref.dtype)

    acc = jnp.dot(a_ref[...], wb_ref[...], preferred_element_type=jnp.float32)
    if first_block_scale is not None:
        acc = acc * jnp.where(pl.program_id(0) == 0, first_block_scale, 1.0)
    o_ref[...] = acc.astype(o_ref.dtype)


def _in_proj(a, w, col0, ncols, out_dtype, name, first_block_scale=None, rotate=0, tm=1024, tn=1024):
    m, k = a.shape
    assert col0 % tn == 0 and ncols % tn == 0 and m % tm == 0
    j0 = col0 // tn
    nblk = ncols // tn
    out_blk = lambda j: jnp.where(j < rotate, j + nblk - rotate, j - rotate)
    return pl.pallas_call(
        functools.partial(_in_proj_kernel, first_block_scale=first_block_scale),
        grid=(nblk, m // tm),
        in_specs=[pl.BlockSpec((tm, k), lambda j, i: (i, 0)),
                  pl.BlockSpec((k, tn), lambda j, i: (0, j0 + j))],
        out_specs=pl.BlockSpec((tm, tn), lambda j, i: (i, out_blk(j))),
        out_shape=jax.ShapeDtypeStruct((m, ncols), out_dtype),
        scratch_shapes=[pltpu.VMEM((k, tn), a.dtype)],
        compiler_params=_params("arbitrary", "arbitrary"),
        name=name,
    )(a, w)


LANES = 128


def _na_bias_kernel(rpb_ref, o_ref):
    var = pl.program_id(0)
    q = lax.broadcasted_iota(jnp.int32, (GRID_W, LANES), 0)
    lane = lax.broadcasted_iota(jnp.int32, (GRID_W, LANES), 1)
    k = lane & (GRID_W - 1)
    start = jnp.clip(q - NA_WIN_COLS // 2, 0, GRID_W - NA_WIN_COLS)
    inside = (k >= start) & (k < start + NA_WIN_COLS)
    left = lane < GRID_W

    def toeplitz(h, j, lane0):
        row = rpb_ref[h, pl.ds(NA_WIN_ROWS - 1 - var + j, 1), :]
        spread = jnp.broadcast_to(row, (GRID_W, LANES))
        shift = (lane0 - (NA_WIN_COLS - 1)) % LANES
        return pltpu.roll(spread, shift, 1, stride=1, stride_axis=0)

    for h in range(NA_HEADS):
        for j in range(0, NA_WIN_ROWS, 2):
            tile = jnp.where(left, toeplitz(h, j, 0), toeplitz(h, j + 1, GRID_W))
            o_ref[0, h, :, j * GRID_W:(j + 2) * GRID_W] = jnp.where(inside, tile * LOG2_E, MASK_VALUE)


def _na_bias_table(rpb):
    heads, nrow, ncol = rpb.shape
    band = NA_WIN_ROWS * GRID_W
    rpb_wide = jnp.pad(rpb.astype(jnp.float32), ((0, 0), (0, 0), (0, LANES - ncol)))
    return pl.pallas_call(
        _na_bias_kernel,
        grid=(NA_WIN_ROWS,),
        in_specs=[pl.BlockSpec((heads, nrow, LANES), lambda v: (0, 0, 0))],
        out_specs=pl.BlockSpec((1, heads, GRID_W, band), lambda v: (v, 0, 0, 0)),
        out_shape=jax.ShapeDtypeStruct((NA_WIN_ROWS, heads, GRID_W, band), jnp.float32),
        compiler_params=_params("parallel"),
        name="na_bias_table",
    )(rpb_wide)


NA_ROWS_PER_STEP = 8


def _na_kernel(q_ref, k0_ref, v0_ref, *refs, rows):
    n = NA_ROWS_PER_STEP
    kn_refs, vn_refs, z_ref = refs[:n], refs[n:2 * n], refs[2 * n]
    bias_refs = refs[2 * n + 1:3 * n + 1]
    o_ref, k_ring, v_ring, s_ref, m_ref = refs[3 * n + 1:]
    lanes = 2 * NA_HEAD_DIM
    band = NA_WIN_ROWS * GRID_W
    half = NA_WIN_ROWS // 2
    first = lax.broadcasted_iota(jnp.int32, (GRID_W, lanes), 1) < NA_HEAD_DIM
    pairs = NA_HEADS // 2
    ones = jnp.ones((band, lanes), jnp.bfloat16)

    step = pl.program_id(0)

    @pl.when(step == 0)
    def _():
        k_ring[...] = k0_ref[...]
        v_ring[...] = v0_ref[...]

    def row_state(i):
        r = step * NA_ROWS_PER_STEP + i
        enter = jnp.clip(r + half - 1, NA_WIN_ROWS - 1, rows - 1) & (NA_WIN_ROWS - 1)
        start = jnp.clip(r - half, 0, rows - NA_WIN_ROWS)
        return enter, [(start + j) & (NA_WIN_ROWS - 1) for j in range(NA_WIN_ROWS)]

    def score_phase(i, slots):
        qrows = slice(i * GRID_W, (i + 1) * GRID_W)
        for p in range(pairs):
            sl = slice(p * lanes, (p + 1) * lanes)
            qp = q_ref[qrows, sl]
            zero = jnp.zeros_like(qp)
            q_bd = jnp.concatenate([jnp.where(first, qp, zero), jnp.where(first, zero, qp)], axis=0)
            k_band = jnp.concatenate([k_ring[j, :, sl] for j in slots], axis=0)
            m = None
            for c0 in range(0, band, 2 * lanes):
                cs = slice(c0, c0 + 2 * lanes)
                s = lax.dot_general(q_bd, k_band[cs], (((1,), (1,)), ((), ())),
                                    preferred_element_type=jnp.float32) + bias_refs[i][0, p, :, cs]
                s_ref[i, p, :, cs] = s
                mh = jnp.max(s, axis=-1, keepdims=True)
                m = mh if m is None else jnp.maximum(m, mh)
            m_ref[i, p] = jnp.broadcast_to(m, (2 * GRID_W, lanes))

    def value_phase(i, slots):
        qrows = slice(i * GRID_W, (i + 1) * GRID_W)
        for p in range(pairs):
            sl = slice(p * lanes, (p + 1) * lanes)
            m = m_ref[i, p]
            e = jnp.exp2(s_ref[i, p] - jnp.concatenate([m] * (band // lanes), axis=-1))
            v_band = jnp.concatenate([v_ring[j, :, sl] for j in slots], axis=0)
            v_ext = jnp.concatenate([v_band, ones], axis=1)
            ol = jnp.dot(e.astype(jnp.bfloat16), v_ext, preferred_element_type=jnp.float32)
            o = ol[:, :lanes] / ol[:, lanes:]
            o = jnp.where(first, o[:GRID_W], o[GRID_W:])
            z = z_ref[qrows, sl]
            o_ref[qrows, sl] = (o * (z * jax.nn.sigmoid(z))).astype(o_ref.dtype)

    states = [row_state(i) for i in range(NA_ROWS_PER_STEP)]
    for i, (enter, slots) in enumerate(states):
        k_ring[enter] = kn_refs[i][0]
        score_phase(i, slots)
        if i > 0:
            value_phase(i - 1, states[i - 1][1])
        v_ring[enter] = vn_refs[i][0]
    value_phase(NA_ROWS_PER_STEP - 1, states[-1][1])


def _neighbourhood_attention(qkv, gates, bias_tab):
    t = qkv.shape[0]
    rows = t // GRID_W
    band = NA_WIN_ROWS * GRID_W

    def row_start(r):
        return jnp.clip(r - NA_WIN_ROWS // 2, 0, rows - NA_WIN_ROWS)

    n = NA_ROWS_PER_STEP
    assert rows % n == 0
    row_blk = (GRID_W, NA_WIDTH)
    step_blk = (n * GRID_W, NA_WIDTH)
    ring = (NA_WIN_ROWS, GRID_W, NA_WIDTH)
    half = NA_WIN_ROWS // 2
    first_rows = [pl.BlockSpec(ring, lambda s, c=c: (0, 0, c), pipeline_mode=pl.Buffered(1)) for c in (1, 2)]
    entering = [pl.BlockSpec((1,) + row_blk,
                             lambda s, c=c, i=i: (jnp.clip(s * n + i + half - 1, NA_WIN_ROWS - 1, rows - 1), 0, c))
                for c in (1, 2) for i in range(n)]
    bias = [pl.BlockSpec((1, NA_HEADS // 2, 2 * GRID_W, band),
                         lambda s, i=i: (s * n + i - row_start(s * n + i), 0, 0, 0)) for i in range(n)]
    qkv3 = qkv.reshape(rows, GRID_W, QKV_COLS)
    bias_tab = bias_tab.reshape(NA_WIN_ROWS, NA_HEADS // 2, 2 * GRID_W, band)
    return pl.pallas_call(
        functools.partial(_na_kernel, rows=rows),
        grid=(rows // n,),
        in_specs=([pl.BlockSpec(step_blk, lambda s: (s, 0))] + first_rows + entering
                  + [pl.BlockSpec(step_blk, lambda s: (s, 0))] + bias),
        out_specs=pl.BlockSpec(step_blk, lambda s: (s, 0)),
        out_shape=jax.ShapeDtypeStruct((t, NA_WIDTH), jnp.bfloat16),
        scratch_shapes=[pltpu.VMEM(ring, qkv.dtype), pltpu.VMEM(ring, qkv.dtype),
                        pltpu.VMEM((n, NA_HEADS // 2, 2 * GRID_W, band), jnp.float32),
                        pltpu.VMEM((n, NA_HEADS // 2, 2 * GRID_W, 2 * NA_HEAD_DIM), jnp.float32)],
        compiler_params=_params("arbitrary"),
        name="neighbourhood_attention",
    )(qkv, qkv3, qkv3, *([qkv3] * (2 * n)), gates, *([bias_tab] * n))


HG_CHUNK = 128
HG_BLOCK = 256


SUBLANES = 8


def _hgrn_level_masks():
    c = HG_CHUNK
    t = np.arange(c)[:, None]
    s = np.arange(c)[None, :]
    out = []
    for reverse in (False, True):
        tt, ss = (c - 1 - t, c - 1 - s) if reverse else (t, s)
        x = tt ^ ss
        levels = [tt == ss]
        m = 1
        while m < c:
            levels.append((tt > ss) & (x >= m) & (x < 2 * m))
            m *= 2
        if len(levels) % 2:
            levels.append(np.zeros((c, c), bool))
        out.append(np.stack([np.concatenate(levels[i:i + 2], axis=1) for i in range(0, len(levels), 2)]))
    return np.stack(out).astype(np.float32)


def _hgrn_chunk(q, fl, v, lb, mask_ref, reverse):
    c, sub = HG_CHUNK, SUBLANES
    nv = c // sub
    bf16 = jnp.bfloat16
    half_t = 0.5 * jnp.tanh(0.5 * fl)
    f = lb + (1.0 - lb) * (0.5 + half_t)
    g = jnp.log2(f)
    k = (1.0 - lb) * (0.5 - half_t)

    srow = lax.broadcasted_iota(jnp.int32, (sub, HG_DIM), 0)
    cp = (sub - 1 - srow) if reverse else srow

    def prev_shift(x, j):
        return pltpu.roll(x, (sub - j) if reverse else j, 0)

    def row_bcast(x, p):
        i = (sub - 1 - p) if reverse else p
        return jnp.broadcast_to(x[i:i + 1, :], (sub, HG_DIM))

    def rows(lst):
        return jnp.concatenate(lst[::-1] if reverse else lst, axis=0)

    blocks = range(nv - 1, -1, -1) if reverse else range(nv)
    gs = [g[b * sub:(b + 1) * sub] for b in blocks]
    loc = []
    for x in gs:
        j = 1
        while j < sub:
            x = x + jnp.where(cp >= j, prev_shift(x, j), 0.0)
            j *= 2
        loc.append(x)
    carry = [None, row_bcast(loc[0], sub - 1)]
    a = [loc[0]]
    for i in range(1, nv):
        a.append(loc[i] + carry[i])
        carry.append(carry[i] + row_bcast(loc[i], sub - 1))
    total = carry[nv]

    def level_exponents(m):
        if m >= sub:
            w = m // sub
            out = []
            for i in range(nv):
                mid = carry[(i // (2 * w)) * (2 * w) + w]
                out.append(a[i] - mid if (i // w) % 2 else mid - a[i])
            return out
        if m == 1:
            return [jnp.where((cp & 1) == 1, x, 0.0) for x in gs]
        out = []
        for x in a:
            r = row_bcast(x, m - 1)
            for blk in range(1, sub // (2 * m)):
                r = jnp.where(cp < blk * 2 * m, r, row_bcast(x, blk * 2 * m + m - 1))
            out.append(-jnp.abs(x - r))
        return out

    d = 1 if reverse else 0
    qb, kb = q.astype(bf16), k.astype(bf16)
    groups = [(qb, kb)]
    m = 1
    while m < c:
        e = jnp.exp2(rows(level_exponents(m))).astype(bf16)
        groups.append((qb * e, kb * e))
        m *= 2

    zeros = jnp.zeros((c, HG_DIM), bf16)
    scores = None
    for slab in range(0, len(groups), 2):
        qa, ka = groups[slab]
        if slab + 1 < len(groups):
            qn, kn = groups[slab + 1]
            lhs = jnp.concatenate([qa, qn], axis=1)
            rhs = jnp.concatenate([jnp.concatenate([ka, zeros], axis=1),
                                   jnp.concatenate([zeros, kn], axis=1)], axis=0)
        else:
            lhs, rhs = qa, jnp.concatenate([ka, zeros], axis=0)
        part = mask_ref[d, slab // 2] * lax.dot_general(lhs, rhs, (((1,), (1,)), ((), ())),
                                                         preferred_element_type=jnp.float32)
        scores = part if scores is None else scores + part

    q_in = (q * jnp.exp2(rows(a))).astype(bf16)
    k_out = (k * jnp.exp2(rows([total - x for x in a]))).astype(bf16)
    vb = v.astype(bf16)
    return (scores.astype(bf16), q_in, k_out, jnp.exp2(total[0:1, :]),
            jnp.concatenate([vb, vb], axis=0), v.T.astype(bf16))


def _hgrn_chunk_state(intra, st):
    scores, q_in, k_out, decay, vv, vtb = intra
    o = jnp.dot(scores, vv, preferred_element_type=jnp.float32)
    o = o + lax.dot_general(q_in, st.astype(jnp.bfloat16), (((1,), (1,)), ((), ())),
                            preferred_element_type=jnp.float32)
    st_new = st * decay + jnp.dot(vtb, k_out, preferred_element_type=jnp.float32)
    return o, st_new


def _hgrn_kernel(lbl_ref, mask_ref, qf_ref, ff_ref, vf_ref, qb_ref, fb_ref, vb_ref,
                 of_ref, ob_ref, st_ref, *, layer):
    @pl.when(pl.program_id(0) == 0)
    def _():
        st_ref[...] = jnp.zeros_like(st_ref)

    lg = lbl_ref[...]
    mx = jnp.max(lg, axis=0, keepdims=True)
    ex = jnp.exp(lg - mx)
    lb = jnp.sum(ex[: layer + 1], axis=0) / jnp.sum(ex, axis=0)

    n = HG_BLOCK // HG_CHUNK
    refs = ((qf_ref, ff_ref, vf_ref, of_ref), (qb_ref, fb_ref, vb_ref, ob_ref))
    state = {}
    pending = None

    def finish(job):
        d, h, ci, rows, cols, intra = job
        st = st_ref[d, h] if ci == 0 else state[d, h]
        o, st = _hgrn_chunk_state(intra, st)
        refs[d][3][rows, cols] = o.astype(refs[d][3].dtype)
        if ci == n - 1:
            st_ref[d, h] = st
        else:
            state[d, h] = st

    for ci in range(n):
        for h in range(HG_HEADS):
            cols = slice(h * HG_DIM, (h + 1) * HG_DIM)
            for d in (0, 1):
                blk = ci if d == 0 else n - 1 - ci
                rows = slice(blk * HG_CHUNK, (blk + 1) * HG_CHUNK)
                q_ref, f_ref, v_ref, _ = refs[d]
                intra = _hgrn_chunk(q_ref[rows, cols], f_ref[rows, cols], v_ref[rows, cols],
                                    lb[d:d + 1, cols], mask_ref, d == 1)
                if pending is not None:
                    finish(pending)
                pending = (d, h, ci, rows, cols, intra)
    finish(pending)


def _hgrn2(rest, lb_logits, layer):
    t = rest.shape[0]
    nb = t // HG_BLOCK
    layers = lb_logits.shape[0]
    masks = jnp.asarray(_hgrn_level_masks())
    q_c, ff_c, fb_c, i_c = 1, 2, 3, 4
    blk = (HG_BLOCK, HG_WIDTH)
    fwd = lambda c: pl.BlockSpec(blk, lambda b: (b, c))
    bwd = lambda c: pl.BlockSpec(blk, lambda b: (nb - 1 - b, c))
    return pl.pallas_call(
        functools.partial(_hgrn_kernel, layer=layer),
        grid=(nb,),
        in_specs=[pl.BlockSpec((layers, 2, HG_WIDTH), lambda b: (0, 0, 0)),
                  pl.BlockSpec(masks.shape, lambda b: (0, 0, 0, 0)),
                  fwd(q_c), fwd(ff_c), fwd(i_c), bwd(q_c), bwd(fb_c), bwd(i_c)],
        out_specs=[pl.BlockSpec(blk, lambda b: (b, 0)),
                   pl.BlockSpec(blk, lambda b: (nb - 1 - b, 0))],
        out_shape=[jax.ShapeDtypeStruct((t, HG_WIDTH), jnp.bfloat16)] * 2,
        scratch_shapes=[pltpu.VMEM((2, HG_HEADS, HG_DIM, HG_DIM), jnp.float32)],
        compiler_params=_params("arbitrary"),
        name="hgrn2_bidirectional",
    )(lb_logits, masks, rest, rest, rest, rest, rest, rest)


def _merge_kernel(oa_ref, of_ref, ob_ref, zb_ref, g_ref, bg_ref, hgw_ref, w_ref, y_ref, wb16_ref):
    f32 = jnp.float32
    d = y_ref.shape[1]

    @pl.when(pl.program_id(0) == 0)
    def _():
        wb16_ref[...] = w_ref[...].astype(wb16_ref.dtype)

    wa_ref = wb16_ref.at[:NA_WIDTH]
    wb_ref = wb16_ref.at[NA_WIDTH:]
    zb = zb_ref[...].astype(f32)
    gate_b = zb * jax.nn.sigmoid(zb)
    osum = of_ref[...].astype(f32) + ob_ref[...].astype(f32)
    parts = []
    for h in range(HG_HEADS):
        sl = slice(h * HG_DIM, (h + 1) * HG_DIM)
        oh = osum[:, sl]
        ms = jnp.mean(oh * oh, axis=-1, keepdims=True)
        parts.append(oh * lax.rsqrt(ms + NORM_EPS) * hgw_ref[:, sl])
    o_b = (jnp.concatenate(parts, axis=-1) * gate_b).astype(jnp.bfloat16)

    pa = jnp.dot(oa_ref[...], wa_ref[...], preferred_element_type=f32)
    pb = jnp.dot(o_b, wb_ref[...], preferred_element_type=f32)
    y = (jax.nn.sigmoid(g_ref[:, :d].astype(f32) + bg_ref[:, :d]) * pa
         + jax.nn.sigmoid(g_ref[:, d:].astype(f32) + bg_ref[:, d:]) * pb)
    y_ref[...] = y.astype(y_ref.dtype)


def _out_proj_kernel(x_ref, y_ref, wo_ref, pw_ref, o_ref, wb_ref):
    @pl.when(pl.program_id(0) == 0)
    def _():
        wb_ref[...] = wo_ref[...].astype(wb_ref.dtype)

    u = jnp.dot(y_ref[...], wb_ref[...], preferred_element_type=jnp.float32)
    ms = jnp.mean(u * u, axis=-1, keepdims=True)
    o_ref[...] = x_ref[...] + u * lax.rsqrt(ms + NORM_EPS) * pw_ref[...]


def _const_spec(shape, r=0, c=0):
    return pl.BlockSpec(shape, lambda i: (r, c), pipeline_mode=pl.Buffered(1))


def _merge(oa, o_f, o_b, gz, b_gate, hg_norm_w, w_br, tm=512):
    t, d = oa.shape[0], w_br.shape[1]
    tile = lambda w, c: pl.BlockSpec((tm, w), lambda i: (i, c))
    return pl.pallas_call(
        _merge_kernel,
        grid=(t // tm,),
        in_specs=[tile(NA_WIDTH, 0), tile(HG_WIDTH, 0), tile(HG_WIDTH, 0),
                  tile(HG_WIDTH, 2 * d // HG_WIDTH), tile(2 * d, 0),
                  _const_spec((1, 2 * d)), _const_spec((1, HG_WIDTH)), _const_spec(w_br.shape)],
        out_specs=tile(d, 0),
        out_shape=jax.ShapeDtypeStruct((t, d), jnp.bfloat16),
        scratch_shapes=[pltpu.VMEM(w_br.shape, oa.dtype)],
        compiler_params=_params("arbitrary"),
        name="branch_merge",
    )(oa, o_f, o_b, gz, gz, b_gate.reshape(1, 2 * d), hg_norm_w.reshape(1, HG_WIDTH), w_br)


def _out_proj(x, y, w_o, post_w, tm=512):
    t, d = x.shape
    tile = pl.BlockSpec((tm, d), lambda i: (i, 0))
    return pl.pallas_call(
        _out_proj_kernel,
        grid=(t // tm,),
        in_specs=[tile, tile, _const_spec((d, d)), _const_spec((1, d))],
        out_specs=tile,
        out_shape=jax.ShapeDtypeStruct((t, d), jnp.float32),
        scratch_shapes=[pltpu.VMEM((d, d), y.dtype)],
        compiler_params=_params("arbitrary"),
        name="out_proj_norm",
    )(x, y, w_o, post_w.reshape(1, d))


def kernel(x, norm_pre_w, w_in, b_gate, na_rel_bias, hg_lb_logits, hg_norm_w, w_branch, w_out, norm_post_w):
    b, t, d = x.shape
    depth = w_in.shape[0]
    bf16 = jnp.bfloat16
    outs = []
    for bi in range(b):
        xb = x[bi]
        for l in range(depth):
            xn = _rmsnorm(xb, norm_pre_w[l])
            qkv = _in_proj(xn, w_in[l], 0, QKV_COLS, bf16, "in_proj_qkv",
                           first_block_scale=LOG2_E * NA_HEAD_DIM ** -0.5, tm=2048)
            rest = _in_proj(xn, w_in[l], QKV_COLS, NA_WIDTH + 4 * HG_WIDTH, jnp.float32, "in_proj_rest")
            gz = _in_proj(xn, w_in[l], QKV_COLS + NA_WIDTH + 4 * HG_WIDTH, HG_WIDTH + 2 * d, bf16,
                          "in_proj_gates", rotate=1, tm=2048)
            oa = _neighbourhood_attention(qkv, rest, _na_bias_table(na_rel_bias[l]))
            o_f, o_b = _hgrn2(rest, hg_lb_logits.astype(jnp.float32), l)
            y = _merge(oa, o_f, o_b, gz, b_gate[l], hg_norm_w[l], w_branch[l])
            xb = _out_proj(xb, y, w_out[l], norm_post_w[l])
        outs.append(xb)
    return outs[0][None] if b == 1 else jnp.stack(outs, axis=0)
```
